```python
import math
import jax, jax.numpy as jnp
from jax import lax
import numpy as np

D_MODEL = 1024
BATCH = 4
SEQ = 4096
DEPTH = 4

GRID_W = 64
CTX_LEN = 256
RET_HEADS = 4
RET_DK = 128
RET_DV = 256
RET_CHUNK = 128
DIFF_HEADS = 4
DIFF_DK = 64
DIFF_DV = 2 * DIFF_DK
DIFF_QBLOCK = 128
GQA_HEADS = 8
GQA_KV_HEADS = 2
GQA_DH = 64
WINDOW = 128
N_EXPERTS = 32
TOP_K = 4
D_EXPERT = D_MODEL
SWIGLU_ALPHA = 1.702
SWIGLU_LIMIT = 7.0
MOE_BLOCK = 256
ROPE_BASE = 10000.0
NORM_EPS = 1e-6

RET_QK_W = RET_HEADS * RET_DK
RET_V_W = RET_HEADS * RET_DV
DIFF_QK_W = DIFF_HEADS * 2 * DIFF_DK
DIFF_V_W = DIFF_HEADS * DIFF_DV
GQA_Q_W = GQA_HEADS * GQA_DH
GQA_KV_W = GQA_KV_HEADS * GQA_DH
N_BRANCH = 3
IN_SPLITS = (RET_QK_W, RET_QK_W, RET_V_W, RET_V_W,
             DIFF_QK_W, DIFF_QK_W, DIFF_V_W,
             GQA_Q_W, GQA_KV_W, GQA_KV_W,
             N_BRANCH * D_MODEL)
D_IN = sum(IN_SPLITS)
BRANCH_W = (RET_V_W, DIFF_V_W, GQA_Q_W)
D_BRANCH = sum(BRANCH_W)

kernel_name = "hybrid_retention_diffattn_swa_moe_dit"


def _offsets(sizes):
    return [int(i) for i in np.cumsum(sizes)[:-1]]


def rms_norm(x, g):
    xf = x.astype(jnp.float32)
    y = xf * lax.rsqrt(jnp.mean(xf * xf, axis=-1, keepdims=True) + NORM_EPS)
    return (y * g.astype(jnp.float32)).astype(x.dtype)


def split_heads(t, n_heads):
    b, l, _ = t.shape
    return t.reshape(b, l, n_heads, -1).transpose(0, 2, 1, 3)


def merge_heads(t):
    b, h, l, d = t.shape
    return t.transpose(0, 2, 1, 3).reshape(b, l, h * d)


def flip_seq(t):
    return t[:, :, ::-1]


def axial_rope(rows, head_dim):
    quarter = head_dim // 4
    inv_freq = ROPE_BASE ** (-jnp.arange(quarter, dtype=jnp.float32) / quarter)
    row = jnp.repeat(jnp.arange(rows, dtype=jnp.float32), GRID_W)
    col = jnp.tile(jnp.arange(GRID_W, dtype=jnp.float32), rows)
    ang = jnp.concatenate([row[:, None] * inv_freq, col[:, None] * inv_freq], axis=-1)
    return jnp.cos(ang), jnp.sin(ang)


def apply_rope(x, cos, sin):
    half = x.shape[-1] // 2
    x1, x2 = x[..., :half], x[..., half:]
    c = cos.astype(x.dtype)
    s = sin.astype(x.dtype)
    return jnp.concatenate([x1 * c - x2 * s, x1 * s + x2 * c], axis=-1)


def retention_chunkwise(q, k, v, log_gamma, s0):
    b, h, l, _ = q.shape
    dv = v.shape[-1]
    C = RET_CHUNK
    n = l // C
    f32 = jnp.float32
    pos = jnp.arange(C, dtype=f32)
    lg = log_gamma[:, None]
    rel = pos[:, None] - pos[None, :]
    decay_mask = jnp.where(rel >= 0, jnp.exp(jnp.maximum(rel, 0.0)[None] * lg[:, :, None]), 0.0)
    q_decay = jnp.exp((pos + 1.0)[None] * lg)[..., None]
    k_decay = jnp.exp((C - 1.0 - pos)[None] * lg)[..., None]
    chunk_decay = jnp.exp(C * lg)[..., None]

    def chunks(t):
        return jnp.moveaxis(t.astype(f32).reshape(b, h, n, C, t.shape[-1]), 2, 0)

    def step(s, qkv):
        qc, kc, vc = qkv
        o = (jnp.einsum('bhij,bhje->bhie', jnp.einsum('bhid,bhjd->bhij', qc, kc) * decay_mask, vc)
             + jnp.einsum('bhid,bhde->bhie', qc * q_decay, s))
        s = s * chunk_decay + jnp.einsum('bhjd,bhje->bhde', kc * k_decay, vc)
        return s, o

    s_fin, o = lax.scan(step, s0, (chunks(q), chunks(k), chunks(v)))
    return jnp.moveaxis(o, 0, 2).reshape(b, h, l, dv), s_fin


def retention_out(o, g):
    mu = jnp.mean(o, axis=-1, keepdims=True)
    var = jnp.mean(jnp.square(o - mu), axis=-1, keepdims=True)
    o = (o - mu) * lax.rsqrt(var + NORM_EPS)
    return jax.nn.silu(g) * merge_heads(o).astype(g.dtype)


def retention_mixer(lat, ctx, lg_f, lg_b, cos, sin, need_ctx):
    q, k, v, g = lat
    qc, kc, vc, gc = ctx
    scale = RET_DK ** -0.5
    q = apply_rope(split_heads(q, RET_HEADS), cos, sin) * scale
    k = apply_rope(split_heads(k, RET_HEADS), cos, sin)
    v = split_heads(v, RET_HEADS)
    qc = split_heads(qc, RET_HEADS) * scale
    kc = split_heads(kc, RET_HEADS)
    vc = split_heads(vc, RET_HEADS)
    s0 = jnp.zeros((q.shape[0], RET_HEADS, RET_DK, RET_DV), jnp.float32)
    oc_f, sc_f = retention_chunkwise(qc, kc, vc, lg_f, s0)
    oc_b, sc_b = retention_chunkwise(flip_seq(qc), flip_seq(kc), flip_seq(vc), lg_b, s0)
    o_f, _ = retention_chunkwise(q, k, v, lg_f, sc_f)
    o_b, _ = retention_chunkwise(flip_seq(q), flip_seq(k), flip_seq(v), lg_b, sc_b)
    y = retention_out(o_f + flip_seq(o_b), g)
    yc = retention_out(oc_f + flip_seq(oc_b), gc) if need_ctx else None
    return y, yc


def diff_weights_values(q1, q2, k1, k2, v, lam):
    scale = DIFF_DK ** -0.5
    a1 = jax.nn.softmax((jnp.einsum('bhqd,bhkd->bhqk', q1, k1) * scale).astype(jnp.float32), axis=-1)
    a2 = jax.nn.softmax((jnp.einsum('bhqd,bhkd->bhqk', q2, k2) * scale).astype(jnp.float32), axis=-1)
    return jnp.einsum('bhqk,bhkd->bhqd', (a1 - lam * a2).astype(v.dtype), v)


def diff_head_norm(o, g, lam_init):
    of = o.astype(jnp.float32)
    of = of * lax.rsqrt(jnp.mean(of * of, axis=-1, keepdims=True) + NORM_EPS) * g.astype(jnp.float32) * (1.0 - lam_init)
    return merge_heads(of.astype(o.dtype))


def diff_attention(lat, ctx, lam, lam_init, subln_g, cos, sin, need_ctx):
    q, k, v = lat
    qc, kc, vc = ctx

    def halves(t, rope):
        t = split_heads(t, DIFF_HEADS)
        t1, t2 = t[..., :DIFF_DK], t[..., DIFF_DK:]
        if rope:
            t1, t2 = apply_rope(t1, cos, sin), apply_rope(t2, cos, sin)
        return t1, t2

    q1, q2 = halves(q, True)
    k1, k2 = halves(k, True)
    kc1, kc2 = halves(kc, False)
    v = split_heads(v, DIFF_HEADS)
    vc = split_heads(vc, DIFF_HEADS)
    K1 = jnp.concatenate([k1, kc1], axis=2)
    K2 = jnp.concatenate([k2, kc2], axis=2)
    V = jnp.concatenate([v, vc], axis=2)
    B, H, S, _ = q1.shape
    nq = S // DIFF_QBLOCK

    def to_blocks(t):
        return jnp.moveaxis(t.reshape(B, H, nq, DIFF_QBLOCK, DIFF_DK), 2, 0)

    o = lax.map(lambda qq: diff_weights_values(qq[0], qq[1], K1, K2, V, lam), (to_blocks(q1), to_blocks(q2)))
    o = jnp.moveaxis(o, 0, 2).reshape(B, H, S, DIFF_DV)
    y = diff_head_norm(o, subln_g, lam_init)
    yc = None
    if need_ctx:
        qc1, qc2 = halves(qc, False)
        yc = diff_head_norm(diff_weights_values(qc1, qc2, kc1, kc2, vc, lam), subln_g, lam_init)
    return y, yc


def band_blocks(t, w):
    b, h, l, d = t.shape
    tb = jnp.pad(t.reshape(b, h, l // w, w, d), ((0, 0), (0, 0), (1, 1), (0, 0), (0, 0)))
    return jnp.concatenate([tb[:, :, :-2], tb[:, :, 1:-1], tb[:, :, 2:]], axis=3)


def window_gqa(lat, ctx, sink, cos, sin, need_ctx):
    q, k, v = lat
    qc, kc, vc = ctx
    B, S, _ = q.shape
    Hk, G, d, W = GQA_KV_HEADS, GQA_HEADS // GQA_KV_HEADS, GQA_DH, WINDOW
    NB = S // W
    scale = d ** -0.5
    sink_f = sink.astype(jnp.float32).reshape(Hk, G)
    q = apply_rope(q.reshape(B, S, Hk, G, d).transpose(0, 2, 3, 1, 4), cos, sin)
    k = apply_rope(split_heads(k, Hk), cos, sin)
    v = split_heads(v, Hk)
    kc = split_heads(kc, Hk)
    vc = split_heads(vc, Hk)
    n_ctx = kc.shape[2]
    qb = q.reshape(B, Hk, G, NB, W, d)
    kb, vb = band_blocks(k, W), band_blocks(v, W)
    s_loc = jnp.einsum('bkgnqd,bknjd->bkgnqj', qb, kb).astype(jnp.float32) * scale
    s_ctx = jnp.einsum('bkgnqd,bkcd->bkgnqc', qb, kc).astype(jnp.float32) * scale
    qi = jnp.arange(W)[:, None]
    kj = jnp.arange(3 * W)[None, :]
    kpos = (jnp.arange(NB)[:, None, None] - 1) * W + kj[None]
    mask = (jnp.abs(qi - kj + W) <= WINDOW)[None] & (kpos >= 0) & (kpos < S)
    s_sink = jnp.broadcast_to(sink_f[None, :, :, None, None, None], s_ctx.shape[:-1] + (1,))
    p = jax.nn.softmax(jnp.concatenate([jnp.where(mask, s_loc, -jnp.inf), s_ctx, s_sink], axis=-1), axis=-1)
    o = (jnp.einsum('bkgnqj,bknjd->bkgnqd', p[..., :3 * W].astype(v.dtype), vb)
         + jnp.einsum('bkgnqc,bkcd->bkgnqd', p[..., 3 * W:3 * W + n_ctx].astype(v.dtype), vc))
    y = o.reshape(B, Hk, G, S, d).transpose(0, 3, 1, 2, 4).reshape(B, S, Hk * G * d)
    yc = None
    if need_ctx:
        qcg = qc.reshape(B, n_ctx, Hk, G, d).transpose(0, 2, 3, 1, 4)
        sc = jnp.einsum('bkgqd,bkcd->bkgqc', qcg, kc).astype(jnp.float32) * scale
        sc_sink = jnp.broadcast_to(sink_f[None, :, :, None, None], sc.shape[:-1] + (1,))
        pc = jax.nn.softmax(jnp.concatenate([sc, sc_sink], axis=-1), axis=-1)
        oc = jnp.einsum('bkgqc,bkcd->bkgqd', pc[..., :n_ctx].astype(vc.dtype), vc)
        yc = oc.transpose(0, 3, 1, 2, 4).reshape(B, n_ctx, Hk * G * d)
    return y, yc


def merge_branches(ys, gate_pre, w_branch, w_out):
    w_r, w_d, w_g = jnp.split(w_branch, _offsets(BRANCH_W), axis=0)
    g_r, g_d, g_g = jnp.split(jax.nn.sigmoid(gate_pre), N_BRANCH, axis=-1)
    merged = g_r * (ys[0] @ w_r) + g_d * (ys[1] @ w_d) + g_g * (ys[2] @ w_g)
    return merged @ w_out


def hybrid_mixer(h, hc, w_in, lg_f, lg_b, lam, lam_init, subln_g, sink, w_branch, w_out, ropes, need_ctx):
    p = jnp.split(h @ w_in, _offsets(IN_SPLITS), axis=-1)
    pc = jnp.split(hc @ w_in, _offsets(IN_SPLITS), axis=-1)
    y_r, yc_r = retention_mixer(p[0:4], pc[0:4], lg_f, lg_b, *ropes[0], need_ctx)
    y_d, yc_d = diff_attention(p[4:7], pc[4:7], lam, lam_init, subln_g, *ropes[1], need_ctx)
    y_g, yc_g = window_gqa(p[7:10], pc[7:10], sink, *ropes[2], need_ctx)
    out = merge_branches((y_r, y_d, y_g), p[10], w_branch, w_out)
    out_c = merge_branches((yc_r, yc_d, yc_g), pc[10], w_branch, w_out) if need_ctx else None
    return out, out_c


def moe_ffn(h, w_router, b_router, w_gu, b_gu, w_dn, b_dn):
    T, D = h.shape
    A = T * TOP_K
    n_blocks = (A + N_EXPERTS * (MOE_BLOCK - 1) + MOE_BLOCK - 1) // MOE_BLOCK
    P = n_blocks * MOE_BLOCK
    logits = (h @ w_router + b_router).astype(jnp.float32)
    top_val, top_idx = lax.top_k(logits, TOP_K)
    weights = jax.nn.softmax(top_val, axis=-1).reshape(A)
    expert = top_idx.reshape(A)
    token = jnp.arange(A, dtype=jnp.int32) // TOP_K
    order = jnp.argsort(expert)
    e_sorted = expert[order]
    counts = jnp.zeros((N_EXPERTS,), jnp.int32).at[expert].add(1)
    padded = (counts + MOE_BLOCK - 1) // MOE_BLOCK * MOE_BLOCK
    pad_end = jnp.cumsum(padded)
    rank = jnp.arange(A, dtype=jnp.int32) - (jnp.cumsum(counts) - counts)[e_sorted]
    dest = (pad_end - padded)[e_sorted] + rank
    slot_token = jnp.full((P,), T, jnp.int32).at[dest].set(token[order])
    slot_weight = jnp.zeros((P,), jnp.float32).at[dest].set(weights[order])
    block_start = jnp.arange(n_blocks, dtype=jnp.int32) * MOE_BLOCK
    block_expert = jnp.minimum(jnp.searchsorted(pad_end, block_start, side='right'), N_EXPERTS - 1)
    h_pad = jnp.concatenate([h, jnp.zeros((1, D), h.dtype)], axis=0)
    xin = h_pad[slot_token].reshape(n_blocks, MOE_BLOCK, D)

    def expert_block(xs):
        xb, e = xs
        gu = xb @ w_gu[e] + b_gu[e]
        glu = jnp.minimum(gu[:, :D_EXPERT], SWIGLU_LIMIT)
        lin = jnp.clip(gu[:, D_EXPERT:], -SWIGLU_LIMIT, SWIGLU_LIMIT)
        return (glu * jax.nn.sigmoid(SWIGLU_ALPHA * glu) * (lin + 1.0)) @ w_dn[e] + b_dn[e]

    y = lax.map(expert_block, (xin, block_expert)).reshape(P, D) * slot_weight[:, None].astype(h.dtype)
    return jnp.zeros((T + 1, D), h.dtype).at[slot_token].add(y)[:T]


def setup_inputs(seed: int = 0) -> dict:
    key = jax.random.key(seed)
    ks = jax.random.split(key, 32)
    f32 = jnp.float32
    D = D_MODEL

    def nrm(k, shape, s):
        return jax.random.normal(k, shape, f32) * s

    gam = 1.0 - 2.0 ** (-5.0 - jnp.arange(RET_HEADS, dtype=f32))
    ret_logit = jnp.log(gam) - jnp.log1p(-gam)
    kb = jax.random.split(ks[31], N_BRANCH)
    w_branch = jnp.concatenate([nrm(kb[i], (DEPTH, w, D), w ** -0.5) for i, w in enumerate(BRANCH_W)], axis=1)
    return {
        "x": nrm(ks[0], (BATCH, SEQ, D), 1.0),
        "c": nrm(ks[1], (BATCH, D), 1.0),
        "ctx": nrm(ks[2], (BATCH, CTX_LEN, D), 1.0),
        "c_ctx": nrm(ks[3], (D,), 1.0),
        "w_mod": nrm(ks[4], (DEPTH, D, 6 * D), 0.5 * D ** -0.5),
        "b_mod": nrm(ks[5], (DEPTH, 6 * D), 0.02),
        "g_norm_mix": 1.0 + nrm(ks[6], (DEPTH, D), 0.02),
        "g_norm_ffn": 1.0 + nrm(ks[7], (DEPTH, D), 0.02),
        "w_in": nrm(ks[8], (DEPTH, D, D_IN), D ** -0.5),
        "ret_decay_fwd": ret_logit + nrm(ks[9], (DEPTH, RET_HEADS), 0.1),
        "ret_decay_bwd": ret_logit + nrm(ks[10], (DEPTH, RET_HEADS), 0.1),
        "diff_lambda_q1": nrm(ks[11], (DEPTH, DIFF_DK), 0.1),
        "diff_lambda_k1": nrm(ks[12], (DEPTH, DIFF_DK), 0.1),
        "diff_lambda_q2": nrm(ks[13], (DEPTH, DIFF_DK), 0.1),
        "diff_lambda_k2": nrm(ks[14], (DEPTH, DIFF_DK), 0.1),
        "diff_subln_g": 1.0 + nrm(ks[15], (DEPTH, DIFF_DV), 0.02),
        "gqa_sink": nrm(ks[16], (DEPTH, GQA_HEADS), 0.5),
        "w_branch": w_branch,
        "w_out": nrm(ks[17], (DEPTH, D, D), D ** -0.5),
        "w_router": nrm(ks[18], (DEPTH, D, N_EXPERTS), D ** -0.5),
        "b_router": nrm(ks[19], (DEPTH, N_EXPERTS), 0.01),
        "w_gate_up": nrm(ks[20], (DEPTH, N_EXPERTS, D, 2 * D_EXPERT), D ** -0.5),
        "b_gate_up": nrm(ks[21], (DEPTH, N_EXPERTS, 2 * D_EXPERT), 0.01),
        "w_down": nrm(ks[22], (DEPTH, N_EXPERTS, D_EXPERT, D), D_EXPERT ** -0.5),
        "b_down": nrm(ks[23], (DEPTH, N_EXPERTS, D), 0.01),
        "g_final": 1.0 + nrm(ks[24], (D,), 0.02),
    }


def reference(x, c, ctx, c_ctx, w_mod, b_mod, g_norm_mix, g_norm_ffn, w_in, ret_decay_fwd, ret_decay_bwd,
              diff_lambda_q1, diff_lambda_k1, diff_lambda_q2, diff_lambda_k2, diff_subln_g, gqa_sink,
              w_branch, w_out, w_router, b_router, w_gate_up, b_gate_up, w_down, b_down, g_final):
    f32 = jnp.float32
    B, S, D = x.shape
    rows = S // GRID_W
    ropes = (axial_rope(rows, RET_DK), axial_rope(rows, DIFF_DK), axial_rope(rows, GQA_DH))
    xc = ctx
    for l in range(DEPTH):
        need_ctx = l < DEPTH - 1
        mod = jax.nn.silu(c) @ w_mod[l] + b_mod[l]
        mod_c = jax.nn.silu(c_ctx) @ w_mod[l] + b_mod[l]
        sh1, sc1, ga1, sh2, sc2, ga2 = jnp.split(mod[:, None, :], 6, axis=-1)
        sh1c, sc1c, ga1c, sh2c, sc2c, ga2c = jnp.split(mod_c, 6, axis=-1)
        h = rms_norm(x, g_norm_mix[l]) * (1.0 + sc1) + sh1
        hc = rms_norm(xc, g_norm_mix[l]) * (1.0 + sc1c) + sh1c
        lam_init = 0.8 - 0.6 * math.exp(-0.3 * l)
        lam = (jnp.exp(jnp.sum(diff_lambda_q1[l].astype(f32) * diff_lambda_k1[l].astype(f32)))
               - jnp.exp(jnp.sum(diff_lambda_q2[l].astype(f32) * diff_lambda_k2[l].astype(f32))) + lam_init)
        lg_f = jax.nn.log_sigmoid(ret_decay_fwd[l].astype(f32))
        lg_b = jax.nn.log_sigmoid(ret_decay_bwd[l].astype(f32))
        mix, mix_c = hybrid_mixer(h, hc, w_in[l], lg_f, lg_b, lam, lam_init, diff_subln_g[l], gqa_sink[l],
                                  w_branch[l], w_out[l], ropes, need_ctx)
        x = x + ga1 * mix
        h = (rms_norm(x, g_norm_ffn[l]) * (1.0 + sc2) + sh2).reshape(B * S, D)
        moe_args = (w_router[l], b_router[l], w_gate_up[l], b_gate_up[l], w_down[l], b_down[l])
        if need_ctx:
            xc = xc + ga1c * mix_c
            n_c = xc.shape[1]
            hc = (rms_norm(xc, g_norm_ffn[l]) * (1.0 + sc2c) + sh2c).reshape(B * n_c, D)
            y = moe_ffn(jnp.concatenate([h, hc], axis=0), *moe_args)
            x = x + ga2 * y[:B * S].reshape(B, S, D)
            xc = xc + ga2c * y[B * S:].reshape(B, n_c, D)
        else:
            x = x + ga2 * moe_ffn(h, *moe_args).reshape(B, S, D)
    return rms_norm(x, g_final)
```

```python
import functools
import math

import jax
import jax.numpy as jnp
from jax import lax
from jax.experimental import pallas as pl
from jax.experimental.pallas import tpu as pltpu

F32 = jnp.float32
BF16 = jnp.bfloat16

GRID_W = 64
RET_HEADS, RET_DK, RET_DV = 4, 128, 256
DIFF_HEADS, DIFF_DK = 4, 64
GQA_HEADS, GQA_KV_HEADS, GQA_DH = 8, 2, 64
WINDOW = 128
N_EXPERTS, TOP_K = 32, 4
SWIGLU_ALPHA, SWIGLU_LIMIT = 1.702, 7.0
MOE_BLOCK = 256
ROPE_BASE = 10000.0
NORM_EPS = 1e-6
NEG_BIG = -1e30

LANES = 128
P_COLS = 5376
IN_TN = 768
RET_CHUNK = 256
DIFF_TQ, DIFF_TK = 512, 512
GQA_TQ = 256
VMEM_LIMIT = 56 * 1024 * 1024

_PLAIN = ("plain", 1.0)
_GROUP_KINDS = ([("r128", RET_DK ** -0.5)] * 4 + [("r128", 1.0)] * 4 + [_PLAIN] * 16
                + [("r64", DIFF_DK ** -0.5)] * 4 + [("r64", 1.0)] * 4 + [_PLAIN] * 4
                + [("r64", GQA_DH ** -0.5)] * 4 + [("r64", 1.0)] + [_PLAIN])
COL_RET_Q, COL_RET_K, COL_RET_V, COL_RET_G = 0, 512, 1024, 2048
COL_DIFF_Q, COL_DIFF_K, COL_DIFF_V = 3072, 3584, 4096
COL_GQA_Q, COL_GQA_K, COL_GQA_V = 4608, 5120, 5248


def _cparams(sem):
    return pltpu.CompilerParams(dimension_semantics=sem, vmem_limit_bytes=VMEM_LIMIT)


def _sigmoid(x):
    return 1.0 / (1.0 + jnp.exp(-x))


def _dot(a, b):
    return jnp.dot(a, b, preferred_element_type=F32)


def _dot_nt(a, b):
    return lax.dot_general(a, b, (((1,), (1,)), ((), ())), preferred_element_type=F32)


def _dot_tn(a, b):
    return lax.dot_general(a, b, (((0,), (0,)), ((), ())), preferred_element_type=F32)


def _split_bf16(a):
    hi = a.astype(BF16)
    lo = (a - hi.astype(F32)).astype(BF16)
    return hi, lo


def _rms(x, g):
    return x * lax.rsqrt(jnp.mean(x * x, axis=-1, keepdims=True) + NORM_EPS) * g


def _mod_kernel(c_ref, w_ref, b_ref, o_ref):
    c = c_ref[...]
    s = c * _sigmoid(c)
    sh, sl = _split_bf16(s)
    wh, wl = _split_bf16(w_ref[0])
    o_ref[0] = _dot(sh, wh) + (_dot(sl, wh) + _dot(sh, wl)) + b_ref[0]


def _mod_all(cvec, w_mod, b_mod):
    depth, d, n = w_mod.shape
    tn = n // 4
    return pl.pallas_call(
        _mod_kernel,
        grid=(depth, n // tn),
        in_specs=[pl.BlockSpec((8, d), lambda l, j: (0, 0)),
                  pl.BlockSpec((1, d, tn), lambda l, j: (l, 0, j)),
                  pl.BlockSpec((1, 1, tn), lambda l, j: (l, 0, j))],
        out_specs=pl.BlockSpec((1, 8, tn), lambda l, j: (l, 0, j)),
        out_shape=jax.ShapeDtypeStruct((depth, 8, n), F32),
        compiler_params=_cparams(("arbitrary", "arbitrary")),
        name="mod",
    )(cvec, w_mod, b_mod.reshape(depth, 1, n))


def _rope_group(a, kind, scale, a128, b128, a64, b64):
    if kind == "plain":
        return a
    if kind == "r128":
        out = a * a128 + pltpu.roll(a, 64, 1) * b128
    else:
        lane = lax.broadcasted_iota(jnp.int32, (1, LANES), 1)
        first = (lane & 63) < 32
        partner = jnp.where(first, pltpu.roll(a, 96, 1), pltpu.roll(a, 32, 1))
        out = a * a64 + partner * b64
    return out * scale if scale != 1.0 else out


def _inproj_kernel(x_ref, g_ref, sc_ref, sh_ref, w_ref, a128_ref, b128_ref, a64_ref, b64_ref,
                   o_ref, h_scr, acc_scr, *, n_tiles):
    j = pl.program_id(1)

    @pl.when(j == 0)
    def _():
        y = _rms(x_ref[...], g_ref[...])
        h_scr[...] = (y * (1.0 + sc_ref[0]) + sh_ref[0]).astype(BF16)

    acc_scr[...] = _dot(h_scr[...], w_ref[...])
    gpt = IN_TN // LANES
    for jj in range(n_tiles):
        @pl.when(j == jj)
        def _(jj=jj):
            for g in range(gpt):
                kind, scale = _GROUP_KINDS[jj * gpt + g]
                a = acc_scr[:, g * LANES:(g + 1) * LANES]
                out = _rope_group(a, kind, scale, a128_ref[...], b128_ref[...], a64_ref[...], b64_ref[...])
                o_ref[:, g * LANES:(g + 1) * LANES] = out.astype(BF16)


def _inproj(xall, g, mod3, w_bf, tabs, dims):
    tt, d = xall.shape
    tm = dims["tm"]
    n_lat_tiles, tps = dims["n_lat_tiles"], dims["tiles_per_seq"]
    n_tiles = P_COLS // IN_TN

    def mrow(i):
        return jnp.minimum(i // tps, dims["batch"])

    def trow(i):
        return jnp.where(i < n_lat_tiles, i % tps, tps)

    tab_spec = pl.BlockSpec((tm, LANES), lambda i, j: (trow(i), 0))
    return pl.pallas_call(
        functools.partial(_inproj_kernel, n_tiles=n_tiles),
        grid=(tt // tm, n_tiles),
        in_specs=[pl.BlockSpec((tm, d), lambda i, j: (i, 0)),
                  pl.BlockSpec((1, d), lambda i, j: (0, 0)),
                  pl.BlockSpec((1, 1, d), lambda i, j: (mrow(i), 0, 1)),
                  pl.BlockSpec((1, 1, d), lambda i, j: (mrow(i), 0, 0)),
                  pl.BlockSpec((d, IN_TN), lambda i, j: (0, j)),
                  tab_spec, tab_spec, tab_spec, tab_spec],
        out_specs=pl.BlockSpec((tm, IN_TN), lambda i, j: (i, j)),
        out_shape=jax.ShapeDtypeStruct((tt, P_COLS), BF16),
        scratch_shapes=[pltpu.VMEM((tm, d), BF16), pltpu.VMEM((tm, IN_TN), F32)],
        compiler_params=_cparams(("arbitrary", "arbitrary")),
        name="inproj",
    )(xall, g, mod3, mod3, w_bf, *tabs)


def _ret_kernel(lg_ref, q_ref, k_ref, v_ref, g_ref, y_ref, s_scr, of_scr, *, ncc, ncl):
    ph = pl.program_id(1)
    t = pl.program_id(2)
    ch = RET_CHUNK

    @pl.when(t == 0)
    def _():
        s_scr[...] = jnp.zeros_like(s_scr)

    fwd = ph == 0
    ii = lax.broadcasted_iota(jnp.int32, (ch, ch), 0).astype(F32)
    jj = lax.broadcasted_iota(jnp.int32, (ch, ch), 1).astype(F32)
    rel = jnp.where(fwd, ii - jj, jj - ii)
    pos = lax.broadcasted_iota(jnp.int32, (ch, 1), 0).astype(F32)
    qpos = jnp.where(fwd, pos + 1.0, ch - pos)
    kpos = jnp.where(fwd, ch - 1.0 - pos, pos)
    mirror = jnp.where(t < ncc, ncc - 1 - t, ncc + ncl - 1 - (t - ncc))
    row0 = pl.multiple_of(jnp.where(fwd, t, mirror) * ch, ch)

    for h in range(RET_HEADS):
        lg = lg_ref[ph, h]
        dmask = jnp.where(rel >= 0.0, jnp.exp(jnp.maximum(rel, 0.0) * lg), 0.0)
        q = q_ref[:, h * RET_DK:(h + 1) * RET_DK]
        k = k_ref[:, h * RET_DK:(h + 1) * RET_DK]
        v = v_ref[:, h * RET_DV:(h + 1) * RET_DV]
        a = (_dot_nt(q, k) * dmask).astype(BF16)
        qd = (q.astype(F32) * jnp.exp(qpos * lg)).astype(BF16)
        kd = (k.astype(F32) * jnp.exp(kpos * lg)).astype(BF16)
        state = s_scr[h]
        o = _dot(a, v) + _dot(qd, state.astype(BF16))
        s_scr[h] = state * jnp.exp(ch * lg) + _dot_tn(kd, v)
        cs = slice(h * RET_DV, (h + 1) * RET_DV)

        @pl.when(fwd)
        def _(o=o, cs=cs):
            of_scr[pl.ds(row0, ch), cs] = o

        @pl.when(jnp.logical_not(fwd))
        def _(o=o, cs=cs):
            ot = o + of_scr[pl.ds(row0, ch), cs]
            mu = jnp.mean(ot, axis=-1, keepdims=True)
            oc = ot - mu
            var = jnp.mean(oc * oc, axis=-1, keepdims=True)
            on = oc * lax.rsqrt(var + NORM_EPS)
            gate = g_ref[:, cs].astype(F32)
            y_ref[:, cs] = (gate * _sigmoid(gate) * on).astype(BF16)


def _retention(p, lg, dims):
    tt = p.shape[0]
    ch = RET_CHUNK
    b, s, c = dims["batch"], dims["seq"], dims["ctx"]
    ncc, ncl = c // ch, s // ch
    nsteps = ncc + ncl
    ctx_base = (b * s) // ch

    def rblk(bi, ph, t):
        fwd_blk = jnp.where(t < ncc, ctx_base + bi * ncc + t, bi * ncl + (t - ncc))
        bwd_blk = jnp.where(t < ncc, ctx_base + bi * ncc + (ncc - 1 - t), bi * ncl + (ncl - 1 - (t - ncc)))
        return jnp.where(ph == 0, fwd_blk, bwd_blk)

    def oblk(bi, ph, t):
        return rblk(bi, 1, jnp.where(ph == 0, 0, t))

    qk_w = RET_HEADS * RET_DK
    v_w = RET_HEADS * RET_DV
    return pl.pallas_call(
        functools.partial(_ret_kernel, ncc=ncc, ncl=ncl),
        grid=(b, 2, nsteps),
        in_specs=[pl.BlockSpec(memory_space=pltpu.SMEM),
                  pl.BlockSpec((ch, qk_w), lambda bi, ph, t: (rblk(bi, ph, t), COL_RET_Q // qk_w)),
                  pl.BlockSpec((ch, qk_w), lambda bi, ph, t: (rblk(bi, ph, t), COL_RET_K // qk_w)),
                  pl.BlockSpec((ch, v_w), lambda bi, ph, t: (rblk(bi, ph, t), COL_RET_V // v_w)),
                  pl.BlockSpec((ch, v_w), lambda bi, ph, t: (rblk(bi, ph, t), COL_RET_G // v_w))],
        out_specs=pl.BlockSpec((ch, v_w), lambda bi, ph, t: (oblk(bi, ph, t), 0)),
        out_shape=jax.ShapeDtypeStruct((tt, v_w), BF16),
        scratch_shapes=[pltpu.VMEM((RET_HEADS, RET_DK, RET_DV), F32),
                        pltpu.VMEM((nsteps * ch, v_w), F32)],
        compiler_params=_cparams(("arbitrary", "arbitrary", "arbitrary")),
        name="retention",
    )(lg, p, p, p, p)


def _diff_kernel(sc_ref, q_ref, *refs, n_src, tk):
    kv = refs[:2 * n_src]
    g_ref, o_ref = refs[2 * n_src], refs[2 * n_src + 1]
    q = q_ref[...]
    tq = q.shape[0]
    lane = lax.broadcasted_iota(jnp.int32, (1, LANES), 1)
    zero = jnp.zeros_like(q)
    qs = (jnp.where(lane < DIFF_DK, q, zero), jnp.where(lane >= DIFF_DK, q, zero))

    def tile(kt, vt, carry):
        new = []
        for st in range(2):
            m, l, acc = carry[st]
            s = _dot_nt(qs[st], kt)
            mn = jnp.maximum(m, jnp.max(s, axis=-1, keepdims=True))
            p = jnp.exp(s - mn)
            alpha = jnp.exp(m - mn)
            l = alpha * l + jnp.sum(p, axis=-1, keepdims=True)
            acc = alpha * acc + _dot(p.astype(BF16), vt)
            new.append((mn, l, acc))
        return tuple(new)

    init = (jnp.full((tq, 1), NEG_BIG, F32), jnp.zeros((tq, 1), F32), jnp.zeros((tq, LANES), F32))
    carry = (init, init)
    for si in range(n_src):
        k_ref, v_ref = kv[2 * si], kv[2 * si + 1]
        rows = k_ref.shape[0]
        tks = min(tk, rows)
        nt = rows // tks
        if nt == 1:
            carry = tile(k_ref[...], v_ref[...], carry)
        else:
            def body(i, c, k_ref=k_ref, v_ref=v_ref, tks=tks):
                r0 = pl.multiple_of(i * tks, tks)
                return tile(k_ref[pl.ds(r0, tks), :], v_ref[pl.ds(r0, tks), :], c)
            carry = lax.fori_loop(0, nt, body, carry)
    (m1, l1, a1), (m2, l2, a2) = carry
    lam = sc_ref[0]
    o = a1 / l1 - lam * (a2 / l2)
    o = o * lax.rsqrt(jnp.mean(o * o, axis=-1, keepdims=True) + NORM_EPS) * g_ref[...] * sc_ref[1]
    o_ref[...] = o.astype(BF16)


def _diff_attention(p, scal, subln_g, dims, ctx_queries):
    b, s, c = dims["batch"], dims["seq"], dims["ctx"]
    tt = p.shape[0]
    qc0, kc0, vc0 = COL_DIFF_Q // LANES, COL_DIFF_K // LANES, COL_DIFF_V // LANES
    ctx_blk = (b * s) // c
    kctx = pl.BlockSpec((c, LANES), lambda bi, h, qi: (ctx_blk + bi, kc0 + h))
    vctx = pl.BlockSpec((c, LANES), lambda bi, h, qi: (ctx_blk + bi, vc0 + h))
    if ctx_queries:
        tq, nq = c, 1
        q_spec = pl.BlockSpec((tq, LANES), lambda bi, h, qi: (ctx_blk + bi, qc0 + h))
        o_spec = pl.BlockSpec((tq, LANES), lambda bi, h, qi: (bi, h))
        out_rows = b * c
        kv_specs, n_src, kv_args = [kctx, vctx], 1, (p, p)
    else:
        tq = min(DIFF_TQ, s)
        nq = s // tq
        q_spec = pl.BlockSpec((tq, LANES), lambda bi, h, qi: (bi * nq + qi, qc0 + h))
        o_spec = pl.BlockSpec((tq, LANES), lambda bi, h, qi: (bi * nq + qi, h))
        out_rows = b * s
        klat = pl.BlockSpec((s, LANES), lambda bi, h, qi: (bi, kc0 + h))
        vlat = pl.BlockSpec((s, LANES), lambda bi, h, qi: (bi, vc0 + h))
        kv_specs, n_src, kv_args = [klat, vlat, kctx, vctx], 2, (p, p, p, p)
    del tt
    return pl.pallas_call(
        functools.partial(_diff_kernel, n_src=n_src, tk=DIFF_TK),
        grid=(b, DIFF_HEADS, nq),
        in_specs=[pl.BlockSpec(memory_space=pltpu.SMEM), q_spec] + kv_specs
                 + [pl.BlockSpec((1, LANES), lambda bi, h, qi: (0, 0))],
        out_specs=o_spec,
        out_shape=jax.ShapeDtypeStruct((out_rows, DIFF_HEADS * LANES), BF16),
        compiler_params=_cparams(("arbitrary", "arbitrary", "arbitrary")),
        name="diff_ctx" if ctx_queries else "diff_lat",
    )(scal, p, *kv_args, subln_g)


def _roll_bf16(a, shift):
    return pltpu.roll(a.astype(F32), shift, 1).astype(BF16)


def _gqa_core(sink_ref, q_ref, kk, vv, mask, o_ref):
    tq = q_ref.shape[0]
    lane = lax.broadcasted_iota(jnp.int32, (1, LANES), 1)
    half_of_lane = lax.shift_right_logical(lane, 6)
    kk_sw, vv_sw = _roll_bf16(kk, 64), _roll_bf16(vv, 64)
    row = lax.broadcasted_iota(jnp.int32, (2 * tq, 1), 0)
    for kvh in range(GQA_KV_HEADS):
        pairs = (2 * kvh, 2 * kvh + 1)
        outs = []
        for hh in range(2):
            kmat = kk if kvh == hh else kk_sw
            vmat = vv if kvh == hh else vv_sw
            qp = [q_ref[:, p * LANES:(p + 1) * LANES] for p in pairs]
            qs = jnp.concatenate([jnp.where(half_of_lane == hh, x, jnp.zeros_like(x)) for x in qp], axis=0)
            s = _dot_nt(qs, kmat)
            if mask is not None:
                s = jnp.where(mask, s, NEG_BIG)
            sink = jnp.where(row < tq, sink_ref[2 * pairs[0] + hh], sink_ref[2 * pairs[1] + hh])
            m = jnp.maximum(jnp.max(s, axis=-1, keepdims=True), sink)
            pr = jnp.exp(s - m)
            den = jnp.sum(pr, axis=-1, keepdims=True) + jnp.exp(sink - m)
            outs.append(_dot(pr.astype(BF16), vmat) / den)
        for pi, p in enumerate(pairs):
            rs = slice(pi * tq, (pi + 1) * tq)
            o_ref[:, p * LANES:(p + 1) * LANES] = jnp.where(half_of_lane == 0, outs[0][rs], outs[1][rs]).astype(BF16)


def _gqa_lat_kernel(sink_ref, q_ref, kp_ref, ko_ref, kn_ref, kc_ref, vp_ref, vo_ref, vn_ref, vc_ref, o_ref, *, seq):
    tq = q_ref.shape[0]
    qi = pl.program_id(1)
    kk = jnp.concatenate([kp_ref[...], ko_ref[...], kn_ref[...], kc_ref[...]], axis=0)
    vv = jnp.concatenate([vp_ref[...], vo_ref[...], vn_ref[...], vc_ref[...]], axis=0)
    nk = kk.shape[0]
    n_loc = tq + 2 * WINDOW
    r2 = lax.broadcasted_iota(jnp.int32, (2 * tq, nk), 0)
    r = jnp.where(r2 >= tq, r2 - tq, r2)
    c = lax.broadcasted_iota(jnp.int32, (2 * tq, nk), 1)
    kpos = qi * tq - WINDOW + c
    mask = ((jnp.abs(r - c + WINDOW) <= WINDOW) & (kpos >= 0) & (kpos < seq)) | (c >= n_loc)
    _gqa_core(sink_ref, q_ref, kk, vv, mask, o_ref)


def _gqa_ctx_kernel(sink_ref, q_ref, kc_ref, vc_ref, o_ref):
    _gqa_core(sink_ref, q_ref, kc_ref[...], vc_ref[...], None, o_ref)


def _gqa(p, sink, dims, ctx_queries):
    b, s, c = dims["batch"], dims["seq"], dims["ctx"]
    qw = GQA_HEADS * GQA_DH
    qc0, kc0, vc0 = COL_GQA_Q // qw, COL_GQA_K // LANES, COL_GQA_V // LANES
    ctx_blk = (b * s) // c
    sm = pl.BlockSpec(memory_space=pltpu.SMEM)
    if ctx_queries:
        return pl.pallas_call(
            _gqa_ctx_kernel,
            grid=(b,),
            in_specs=[sm, pl.BlockSpec((c, qw), lambda bi: (ctx_blk + bi, qc0)),
                      pl.BlockSpec((c, LANES), lambda bi: (ctx_blk + bi, kc0)),
                      pl.BlockSpec((c, LANES), lambda bi: (ctx_blk + bi, vc0))],
            out_specs=pl.BlockSpec((c, qw), lambda bi: (bi, 0)),
            out_shape=jax.ShapeDtypeStruct((b * c, qw), BF16),
            compiler_params=_cparams(("arbitrary",)),
            name="gqa_ctx",
        )(sink, p, p, p)
    tq = GQA_TQ
    nq = s // tq
    wpq = tq // WINDOW
    nwb = s // WINDOW

    def loc_specs(col):
        return [pl.BlockSpec((WINDOW, LANES), lambda bi, qi: (bi * nwb + jnp.maximum(qi * wpq - 1, 0), col)),
                pl.BlockSpec((tq, LANES), lambda bi, qi: (bi * nq + qi, col)),
                pl.BlockSpec((WINDOW, LANES), lambda bi, qi: (bi * nwb + jnp.minimum(qi * wpq + wpq, nwb - 1), col)),
                pl.BlockSpec((c, LANES), lambda bi, qi: (ctx_blk + bi, col))]

    return pl.pallas_call(
        functools.partial(_gqa_lat_kernel, seq=s),
        grid=(b, nq),
        in_specs=[sm, pl.BlockSpec((tq, qw), lambda bi, qi: (bi * nq + qi, qc0))] + loc_specs(kc0) + loc_specs(vc0),
        out_specs=pl.BlockSpec((tq, qw), lambda bi, qi: (bi * nq + qi, 0)),
        out_shape=jax.ShapeDtypeStruct((b * s, qw), BF16),
        compiler_params=_cparams(("arbitrary", "arbitrary")),
        name="gqa_lat",
    )(sink, p, *([p] * 8))


def _merge_kernel(x_ref, yr_ref, yd_ref, yg_ref, gmix_ref, gffn_ref, sh1_ref, sc1_ref, ga1_ref, sh2_ref, sc2_ref,
                  wg_ref, wb_ref, wo_ref, wrh_ref, wrl_ref, br_ref, x1_ref, h2_ref, ti_ref, tw_ref):
    d = x_ref.shape[1]
    x = x_ref[...]
    h = (_rms(x, gmix_ref[...]) * (1.0 + sc1_ref[0]) + sh1_ref[0]).astype(BF16)
    rw = RET_HEADS * RET_DV
    dw = DIFF_HEADS * LANES
    merged = None
    for bi, (y_ref, r0, r1) in enumerate(((yr_ref, 0, rw), (yd_ref, rw, rw + dw), (yg_ref, rw + dw, wb_ref.shape[0]))):
        gate = _sigmoid(_dot(h, wg_ref[:, bi * d:(bi + 1) * d]))
        term = gate * _dot(y_ref[...], wb_ref[r0:r1, :])
        merged = term if merged is None else merged + term
    x1 = x + ga1_ref[0] * _dot(merged.astype(BF16), wo_ref[...])
    x1_ref[...] = x1
    h2 = _rms(x1, gffn_ref[...]) * (1.0 + sc2_ref[0]) + sh2_ref[0]
    h2_ref[...] = h2
    hh, hl = _split_bf16(h2)
    wrh, wrl = wrh_ref[...], wrl_ref[...]
    logits = _dot_nt(wrh, hh) + (_dot_nt(wrl, hh) + _dot_nt(wrh, hl)) + br_ref[...]
    eidx = lax.broadcasted_iota(jnp.int32, logits.shape, 0).astype(F32)
    vals = []
    for k in range(TOP_K):
        mx = jnp.max(logits, axis=0, keepdims=True)
        ix = jnp.min(jnp.where(logits == mx, eidx, float(N_EXPERTS)), axis=0, keepdims=True)
        vals.append(mx)
        ti_ref[k:k + 1, :] = ix.astype(jnp.int32)
        logits = jnp.where(eidx == ix, -jnp.inf, logits)
    es = [jnp.exp(v - vals[0]) for v in vals]
    tot = es[0] + es[1] + es[2] + es[3]
    for k in range(TOP_K):
        tw_ref[k:k + 1, :] = es[k] / tot


def _merge(xall, yr, yd, yg, gmix, gffn, mod3, wg, wb, wo, wrh, wrl, br, dims, ntok):
    d = xall.shape[1]
    tm = dims["tm_merge"]
    tps = dims["seq"] // tm

    def mrow(i):
        return jnp.minimum(i // tps, dims["batch"])

    def modspec(k):
        return pl.BlockSpec((1, 1, d), lambda i: (mrow(i), 0, k))

    def full(a):
        return pl.BlockSpec(a.shape, lambda i: (0,) * a.ndim)

    def rows(w):
        return pl.BlockSpec((tm, w), lambda i: (i, 0))

    return pl.pallas_call(
        _merge_kernel,
        grid=(ntok // tm,),
        in_specs=[rows(d), rows(yr.shape[1]), rows(yd.shape[1]), rows(yg.shape[1]), full(gmix), full(gffn),
                  modspec(0), modspec(1), modspec(2), modspec(3), modspec(4),
                  full(wg), full(wb), full(wo), full(wrh), full(wrl), full(br)],
        out_specs=[rows(d), rows(d), pl.BlockSpec((TOP_K, tm), lambda i: (0, i)),
                   pl.BlockSpec((TOP_K, tm), lambda i: (0, i))],
        out_shape=[jax.ShapeDtypeStruct((ntok, d), F32), jax.ShapeDtypeStruct((ntok, d), F32),
                   jax.ShapeDtypeStruct((TOP_K, ntok), jnp.int32), jax.ShapeDtypeStruct((TOP_K, ntok), F32)],
        compiler_params=_cparams(("arbitrary",)),
        name="merge",
    )(xall, yr, yd, yg, gmix, gffn, mod3, mod3, mod3, mod3, mod3, wg, wb, wo, wrh, wrl, br)


TOK_BITS = 15


def _moe_kernel(be_ref, nv_ref, nu_ref, code_ref, sw_ref, h2_hbm, wgu_ref, bgu_ref, wdn_ref, bdn_ref, o_hbm,
                xbuf, ybuf, wgu_bf, wdn_bf, gsem, ssem):
    i = pl.program_id(0)
    nu = nu_ref[0]
    slot = i % 2
    nb = MOE_BLOCK
    d = wdn_bf.shape[0]

    def gather_start(blk, sl):
        def body(r, carry):
            tok = code_ref[blk * nb + r] & ((1 << TOK_BITS) - 1)
            pltpu.make_async_copy(h2_hbm.at[pl.ds(tok, 1)], xbuf.at[sl, pl.ds(r, 1)], gsem.at[sl]).start()
            return carry
        lax.fori_loop(0, nv_ref[blk], body, 0)

    def wait_rows(n, copy_of_rows):
        @pl.when(n == nb)
        def _():
            copy_of_rows(nb).wait()

        @pl.when(n != nb)
        def _():
            def body(r, carry):
                copy_of_rows(1).wait()
                return carry
            lax.fori_loop(0, n, body, 0)

    def gather_wait(blk, sl):
        wait_rows(nv_ref[blk], lambda n: pltpu.make_async_copy(
            h2_hbm.at[pl.ds(0, n)], xbuf.at[sl, pl.ds(0, n)], gsem.at[sl]))

    def scatter_start(blk, sl):
        def body(r, carry):
            dst = lax.shift_right_logical(code_ref[blk * nb + r], TOK_BITS)
            pltpu.make_async_copy(ybuf.at[sl, pl.ds(r, 1)], o_hbm.at[pl.ds(dst, 1)], ssem.at[sl]).start()
            return carry
        lax.fori_loop(0, nv_ref[blk], body, 0)

    def scatter_wait(blk, sl):
        wait_rows(nv_ref[blk], lambda n: pltpu.make_async_copy(
            ybuf.at[sl, pl.ds(0, n)], o_hbm.at[pl.ds(0, n)], ssem.at[sl]))

    @pl.when(i < nu)
    def _():
        @pl.when(i == 0)
        def _():
            xbuf[...] = jnp.zeros_like(xbuf)
            gather_start(0, 0)

        @pl.when(i + 1 < nu)
        def _():
            gather_start(i + 1, 1 - slot)

        @pl.when((i == 0) | (be_ref[i] != be_ref[jnp.maximum(i - 1, 0)]))
        def _():
            rc = 128

            def cast(r, carry):
                r0 = pl.multiple_of(r * rc, rc)
                wgu_bf[pl.ds(r0, rc), :] = wgu_ref[0, pl.ds(r0, rc), :].astype(BF16)
                wdn_bf[pl.ds(r0, rc), :] = wdn_ref[0, pl.ds(r0, rc), :].astype(BF16)
                return carry
            lax.fori_loop(0, d // rc, cast, 0)

        gather_wait(i, slot)
        x = xbuf[slot].astype(BF16)
        gu = _dot(x, wgu_bf[...]) + bgu_ref[0]
        glu = jnp.minimum(gu[:, :d], SWIGLU_LIMIT)
        lin = jnp.clip(gu[:, d:], -SWIGLU_LIMIT, SWIGLU_LIMIT)
        act = glu * _sigmoid(SWIGLU_ALPHA * glu) * (lin + 1.0)
        y = (_dot(act.astype(BF16), wdn_bf[...]) + bdn_ref[0]) * sw_ref[0]

        @pl.when(i >= 2)
        def _():
            scatter_wait(i - 2, slot)

        ybuf[slot] = y
        scatter_start(i, slot)

        @pl.when(i == nu - 1)
        def _():
            @pl.when(i >= 1)
            def _():
                scatter_wait(i - 1, 1 - slot)
            scatter_wait(i, slot)


def _moe(h2, route, w_gu, b_gu, w_dn, b_dn):
    codes, slot_w, block_expert, n_valid, n_used = route
    ntok, d = h2.shape
    n_blocks = block_expert.shape[0]
    ne = w_gu.shape[0]

    def wspec(shape):
        return pl.BlockSpec(shape, lambda i, be, nv, nu, cd: (be[i], 0, 0))

    grid_spec = pltpu.PrefetchScalarGridSpec(
        num_scalar_prefetch=4,
        grid=(n_blocks,),
        in_specs=[pl.BlockSpec((1, MOE_BLOCK, 1), lambda i, be, nv, nu, cd: (i, 0, 0)),
                  pl.BlockSpec(memory_space=pl.ANY),
                  wspec((1, d, 2 * d)), wspec((1, 1, 2 * d)), wspec((1, d, d)), wspec((1, 1, d))],
        out_specs=pl.BlockSpec(memory_space=pl.ANY),
        scratch_shapes=[pltpu.VMEM((2, MOE_BLOCK, d), F32), pltpu.VMEM((2, MOE_BLOCK, d), F32),
                        pltpu.VMEM((d, 2 * d), BF16), pltpu.VMEM((d, d), BF16),
                        pltpu.SemaphoreType.DMA((2,)), pltpu.SemaphoreType.DMA((2,))])
    return pl.pallas_call(
        _moe_kernel,
        grid_spec=grid_spec,
        out_shape=jax.ShapeDtypeStruct((TOP_K * ntok, d), F32),
        compiler_params=_cparams(("arbitrary",)),
        name="moe",
    )(block_expert, n_valid, n_used, codes, slot_w, h2, w_gu, b_gu.reshape(ne, 1, 2 * d), w_dn, b_dn.reshape(ne, 1, d))


def _route(ti, tw, ntok):
    a = TOP_K * ntok
    n_blocks = (a + N_EXPERTS * (MOE_BLOCK - 1) + MOE_BLOCK - 1) // MOE_BLOCK
    expert = ti.reshape(a)
    wts = tw.reshape(a)
    order = jnp.argsort(expert).astype(jnp.int32)
    counts = jnp.sum((expert[:, None] == jnp.arange(N_EXPERTS, dtype=jnp.int32)[None, :]).astype(jnp.int32), axis=0)
    padded = (counts + MOE_BLOCK - 1) // MOE_BLOCK * MOE_BLOCK
    pad_end = jnp.cumsum(padded)
    pad_start = pad_end - padded
    cstart = jnp.cumsum(counts) - counts
    n_used = (pad_end[-1:] // MOE_BLOCK).astype(jnp.int32)
    block_start = jnp.arange(n_blocks, dtype=jnp.int32) * MOE_BLOCK
    block_expert = jnp.minimum(jnp.searchsorted(pad_end, block_start, side="right"), N_EXPERTS - 1).astype(jnp.int32)
    n_valid = jnp.clip(counts[block_expert] - (block_start - pad_start[block_expert]), 0, MOE_BLOCK).astype(jnp.int32)
    e_s = jnp.repeat(block_expert, MOE_BLOCK)
    slot = jnp.arange(n_blocks * MOE_BLOCK, dtype=jnp.int32)
    r = slot - pad_start[e_s]
    valid = r < counts[e_s]
    src = jnp.where(valid, order[jnp.clip(cstart[e_s] + r, 0, a - 1)], 0)
    codes = lax.shift_left(src, TOK_BITS) | (src % ntok)
    slot_w = jnp.where(valid, wts[src], 0.0).reshape(n_blocks, MOE_BLOCK, 1)
    return codes, slot_w, block_expert, n_valid, n_used


def _combine_kernel(x1_ref, o0_ref, o1_ref, o2_ref, o3_ref, ga2_ref, gf_ref, x2_ref, *, final):
    y = (o0_ref[...] + o1_ref[...]) + (o2_ref[...] + o3_ref[...])
    x2 = x1_ref[...] + ga2_ref[0] * y
    x2_ref[...] = _rms(x2, gf_ref[...]) if final else x2


def _combine(x1, o4, mod3, g_final, dims, final):
    ntok, d = x1.shape
    tm = dims["tm_merge"]
    tps = dims["seq"] // tm
    nt = ntok // tm

    def ospec(k):
        return pl.BlockSpec((tm, d), lambda i: (k * nt + i, 0))

    return pl.pallas_call(
        functools.partial(_combine_kernel, final=final),
        grid=(nt,),
        in_specs=[pl.BlockSpec((tm, d), lambda i: (i, 0)), ospec(0), ospec(1), ospec(2), ospec(3),
                  pl.BlockSpec((1, 1, d), lambda i: (jnp.minimum(i // tps, dims["batch"]), 0, 5)),
                  pl.BlockSpec((1, d), lambda i: (0, 0))],
        out_specs=pl.BlockSpec((tm, d), lambda i: (i, 0)),
        out_shape=jax.ShapeDtypeStruct((ntok, d), F32),
        compiler_params=_cparams(("arbitrary",)),
        name="combine",
    )(x1, o4, o4, o4, o4, mod3, g_final)


def _rope_tables(seq, pad_rows):
    pos = jnp.arange(seq, dtype=jnp.int32)
    row = (pos // GRID_W).astype(F32)
    col = (pos % GRID_W).astype(F32)

    def cs(head_dim):
        quarter = head_dim // 4
        inv_freq = ROPE_BASE ** (-jnp.arange(quarter, dtype=F32) / quarter)
        ang = jnp.concatenate([row[:, None] * inv_freq, col[:, None] * inv_freq], axis=-1)
        return jnp.cos(ang), jnp.sin(ang)

    c128, s128 = cs(RET_DK)
    c64, s64 = cs(DIFF_DK)
    ones = jnp.ones((pad_rows, LANES), F32)
    zeros = jnp.zeros((pad_rows, LANES), F32)
    a128 = jnp.concatenate([jnp.concatenate([c128, c128], -1), ones], 0)
    b128 = jnp.concatenate([jnp.concatenate([-s128, s128], -1), zeros], 0)
    a64 = jnp.concatenate([jnp.concatenate([c64, c64, c64, c64], -1), ones], 0)
    b64 = jnp.concatenate([jnp.concatenate([-s64, s64, -s64, s64], -1), zeros], 0)
    return a128, b128, a64, b64


def _pick_tile(cands, *sizes):
    for t in cands:
        if all(s % t == 0 for s in sizes):
            return t
    raise ValueError("no tile size fits")


def kernel(x, c, ctx, c_ctx, w_mod, b_mod, g_norm_mix, g_norm_ffn, w_in, ret_decay_fwd, ret_decay_bwd, diff_lambda_q1, diff_lambda_k1, diff_lambda_q2, diff_lambda_k2, diff_subln_g, gqa_sink, w_branch, w_out, w_router, b_router, w_gate_up, b_gate_up, w_down, b_down, g_final):
    b, s, d = x.shape
    n_ctx = ctx.shape[1]
    depth = w_mod.shape[0]
    n_lat = b * s
    tt = n_lat + b * n_ctx
    assert s % RET_CHUNK == 0 and n_ctx % RET_CHUNK == 0 and s % GQA_TQ == 0 and n_lat % n_ctx == 0
    assert tt < (1 << TOK_BITS) and b < 8
    tm = _pick_tile((1024, 512, 256), s, b * n_ctx)
    tm_merge = _pick_tile((512, 256), s, b * n_ctx)
    dims = dict(batch=b, seq=s, ctx=n_ctx, tm=tm, tm_merge=tm_merge,
                n_lat_tiles=n_lat // tm, tiles_per_seq=s // tm)

    xall = jnp.concatenate([x.reshape(n_lat, d), ctx.reshape(b * n_ctx, d)], axis=0)
    cvec = jnp.concatenate([c, c_ctx[None, :], jnp.zeros((8 - b - 1, d), F32)], axis=0)
    mod = _mod_all(cvec, w_mod, b_mod)
    tabs = _rope_tables(s, tm)

    for l in range(depth):
        last = l == depth - 1
        mod3 = mod[l].reshape(8, 1, 6 * d)
        w_l = w_in[l]
        p = _inproj(xall, g_norm_mix[l][None, :], mod3, w_l[:, :P_COLS].astype(BF16), tabs, dims)

        lg = jnp.stack([jax.nn.log_sigmoid(ret_decay_fwd[l].astype(F32)),
                        jax.nn.log_sigmoid(ret_decay_bwd[l].astype(F32))], axis=0)
        y_r = _retention(p, lg, dims)

        lam_init = 0.8 - 0.6 * math.exp(-0.3 * l)
        lam = (jnp.exp(jnp.sum(diff_lambda_q1[l].astype(F32) * diff_lambda_k1[l].astype(F32)))
               - jnp.exp(jnp.sum(diff_lambda_q2[l].astype(F32) * diff_lambda_k2[l].astype(F32))) + lam_init)
        scal = jnp.stack([lam, jnp.asarray(1.0 - lam_init, F32)])
        subln = diff_subln_g[l][None, :].astype(F32)
        sink = gqa_sink[l].astype(F32)
        y_d = _diff_attention(p, scal, subln, dims, False)
        y_g = _gqa(p, sink, dims, False)
        if not last:
            y_d = jnp.concatenate([y_d, _diff_attention(p, scal, subln, dims, True)], axis=0)
            y_g = jnp.concatenate([y_g, _gqa(p, sink, dims, True)], axis=0)
        ntok = n_lat if last else tt

        wr = w_router[l].T
        wrh, wrl = _split_bf16(wr)
        x1, h2, ti, tw = _merge(xall, y_r, y_d, y_g, g_norm_mix[l][None, :], g_norm_ffn[l][None, :], mod3,
                                w_l[:, P_COLS:].astype(BF16), w_branch[l].astype(BF16), w_out[l].astype(BF16),
                                wrh, wrl, b_router[l][:, None].astype(F32), dims, ntok)
        o4 = _moe(h2, _route(ti, tw, ntok), w_gate_up[l], b_gate_up[l], w_down[l], b_down[l])
        xall = _combine(x1, o4, mod3, g_final[None, :], dims, last)
    return xall.reshape(b, s, d)
```

```python
import functools
import math

import jax
import jax.numpy as jnp
from jax import lax
from jax.experimental import pallas as pl
from jax.experimental.pallas import tpu as pltpu

F32 = jnp.float32
BF16 = jnp.bfloat16

GRID_W = 64
RET_HEADS, RET_DK, RET_DV = 4, 128, 256
DIFF_HEADS, DIFF_DK = 4, 64
GQA_HEADS, GQA_KV_HEADS, GQA_DH = 8, 2, 64
WINDOW = 128
N_EXPERTS, TOP_K = 32, 4
SWIGLU_ALPHA, SWIGLU_LIMIT = 1.702, 7.0
MOE_BLOCK = 256
ROPE_BASE = 10000.0
NORM_EPS = 1e-6
NEG_BIG = -1e30

LANES = 128
P_COLS = 5376
IN_TN = 768
RET_CHUNK = 256
DIFF_TQ, DIFF_TK = 512, 512
DIFF_ROW_CHUNK = 64
LOG2E = math.log2(math.e)
GQA_TQ = 256
VMEM_LIMIT = 56 * 1024 * 1024

_PLAIN = ("plain", 1.0)
_GROUP_KINDS = ([("r128", RET_DK ** -0.5)] * 4 + [("r128", 1.0)] * 4 + [_PLAIN] * 16
                + [("r64", DIFF_DK ** -0.5 * LOG2E)] * 4 + [("r64", 1.0)] * 4 + [_PLAIN] * 4
                + [("r64", GQA_DH ** -0.5)] * 4 + [("r64", 1.0)] + [_PLAIN])
COL_RET_Q, COL_RET_K, COL_RET_V, COL_RET_G = 0, 512, 1024, 2048
COL_DIFF_Q, COL_DIFF_K, COL_DIFF_V = 3072, 3584, 4096
COL_GQA_Q, COL_GQA_K, COL_GQA_V = 4608, 5120, 5248


def _cparams(sem):
    return pltpu.CompilerParams(dimension_semantics=sem, vmem_limit_bytes=VMEM_LIMIT)


def _sigmoid(x):
    return 1.0 / (1.0 + jnp.exp(-x))


def _dot(a, b):
    return jnp.dot(a, b, preferred_element_type=F32)


def _dot_nt(a, b):
    return lax.dot_general(a, b, (((1,), (1,)), ((), ())), preferred_element_type=F32)


def _dot_tn(a, b):
    return lax.dot_general(a, b, (((0,), (0,)), ((), ())), preferred_element_type=F32)


def _split_bf16(a):
    hi = a.astype(BF16)
    lo = (a - hi.astype(F32)).astype(BF16)
    return hi, lo


def _rms(x, g):
    return x * lax.rsqrt(jnp.mean(x * x, axis=-1, keepdims=True) + NORM_EPS) * g


def _mod_kernel(c_ref, w_ref, b_ref, o_ref):
    c = c_ref[...]
    s = c * _sigmoid(c)
    sh, sl = _split_bf16(s)
    wh, wl = _split_bf16(w_ref[0])
    o_ref[0] = _dot(sh, wh) + (_dot(sl, wh) + _dot(sh, wl)) + b_ref[0]


def _mod_all(cvec, w_mod, b_mod):
    depth, d, n = w_mod.shape
    tn = n // 4
    return pl.pallas_call(
        _mod_kernel,
        grid=(depth, n // tn),
        in_specs=[pl.BlockSpec((8, d), lambda l, j: (0, 0)),
                  pl.BlockSpec((1, d, tn), lambda l, j: (l, 0, j)),
                  pl.BlockSpec((1, 1, tn), lambda l, j: (l, 0, j))],
        out_specs=pl.BlockSpec((1, 8, tn), lambda l, j: (l, 0, j)),
        out_shape=jax.ShapeDtypeStruct((depth, 8, n), F32),
        compiler_params=_cparams(("arbitrary", "arbitrary")),
        name="mod",
    )(cvec, w_mod, b_mod.reshape(depth, 1, n))


def _rope_group(a, kind, scale, a128, b128, a64, b64):
    if kind == "plain":
        return a
    if kind == "r128":
        out = a * a128 + pltpu.roll(a, 64, 1) * b128
    else:
        lane = lax.broadcasted_iota(jnp.int32, (1, LANES), 1)
        first = (lane & 63) < 32
        partner = jnp.where(first, pltpu.roll(a, 96, 1), pltpu.roll(a, 32, 1))
        out = a * a64 + partner * b64
    return out * scale if scale != 1.0 else out


def _inproj_kernel(x_ref, g_ref, sc_ref, sh_ref, w_ref, a128_ref, b128_ref, a64_ref, b64_ref,
                   o_ref, h_scr, acc_scr, *, n_tiles):
    j = pl.program_id(1)

    @pl.when(j == 0)
    def _():
        y = _rms(x_ref[...], g_ref[...])
        h_scr[...] = (y * (1.0 + sc_ref[0]) + sh_ref[0]).astype(BF16)

    acc_scr[...] = _dot(h_scr[...], w_ref[...])
    gpt = IN_TN // LANES
    for jj in range(n_tiles):
        @pl.when(j == jj)
        def _(jj=jj):
            for g in range(gpt):
                kind, scale = _GROUP_KINDS[jj * gpt + g]
                a = acc_scr[:, g * LANES:(g + 1) * LANES]
                out = _rope_group(a, kind, scale, a128_ref[...], b128_ref[...], a64_ref[...], b64_ref[...])
                o_ref[:, g * LANES:(g + 1) * LANES] = out.astype(BF16)


def _inproj(xall, g, mod3, w_bf, tabs, dims):
    tt, d = xall.shape
    tm = dims["tm"]
    n_lat_tiles, tps = dims["n_lat_tiles"], dims["tiles_per_seq"]
    n_tiles = P_COLS // IN_TN

    def mrow(i):
        return jnp.minimum(i // tps, dims["batch"])

    def trow(i):
        return jnp.where(i < n_lat_tiles, i % tps, tps)

    tab_spec = pl.BlockSpec((tm, LANES), lambda i, j: (trow(i), 0))
    return pl.pallas_call(
        functools.partial(_inproj_kernel, n_tiles=n_tiles),
        grid=(tt // tm, n_tiles),
        in_specs=[pl.BlockSpec((tm, d), lambda i, j: (i, 0)),
                  pl.BlockSpec((1, d), lambda i, j: (0, 0)),
                  pl.BlockSpec((1, 1, d), lambda i, j: (mrow(i), 0, 1)),
                  pl.BlockSpec((1, 1, d), lambda i, j: (mrow(i), 0, 0)),
                  pl.BlockSpec((d, IN_TN), lambda i, j: (0, j)),
                  tab_spec, tab_spec, tab_spec, tab_spec],
        out_specs=pl.BlockSpec((tm, IN_TN), lambda i, j: (i, j)),
        out_shape=jax.ShapeDtypeStruct((tt, P_COLS), BF16),
        scratch_shapes=[pltpu.VMEM((tm, d), BF16), pltpu.VMEM((tm, IN_TN), F32)],
        compiler_params=_cparams(("arbitrary", "arbitrary")),
        name="inproj",
    )(xall, g, mod3, mod3, w_bf, *tabs)


def _ret_kernel(lg_ref, q_ref, k_ref, v_ref, g_ref, y_ref, s_scr, of_scr, *, ncc, ncl):
    ph = pl.program_id(1)
    t = pl.program_id(2)
    ch = RET_CHUNK

    @pl.when(t == 0)
    def _():
        s_scr[...] = jnp.zeros_like(s_scr)

    fwd = ph == 0
    ii = lax.broadcasted_iota(jnp.int32, (ch, ch), 0).astype(F32)
    jj = lax.broadcasted_iota(jnp.int32, (ch, ch), 1).astype(F32)
    rel = jnp.where(fwd, ii - jj, jj - ii)
    pos = lax.broadcasted_iota(jnp.int32, (ch, 1), 0).astype(F32)
    qpos = jnp.where(fwd, pos + 1.0, ch - pos)
    kpos = jnp.where(fwd, ch - 1.0 - pos, pos)
    mirror = jnp.where(t < ncc, ncc - 1 - t, ncc + ncl - 1 - (t - ncc))
    row0 = pl.multiple_of(jnp.where(fwd, t, mirror) * ch, ch)

    for h in range(RET_HEADS):
        lg = lg_ref[ph, h]
        dmask = jnp.where(rel >= 0.0, jnp.exp(jnp.maximum(rel, 0.0) * lg), 0.0)
        q = q_ref[:, h * RET_DK:(h + 1) * RET_DK]
        k = k_ref[:, h * RET_DK:(h + 1) * RET_DK]
        v = v_ref[:, h * RET_DV:(h + 1) * RET_DV]
        a = (_dot_nt(q, k) * dmask).astype(BF16)
        qd = (q.astype(F32) * jnp.exp(qpos * lg)).astype(BF16)
        kd = (k.astype(F32) * jnp.exp(kpos * lg)).astype(BF16)
        state = s_scr[h]
        o = _dot(a, v) + _dot(qd, state.astype(BF16))
        s_scr[h] = state * jnp.exp(ch * lg) + _dot_tn(kd, v)
        cs = slice(h * RET_DV, (h + 1) * RET_DV)

        @pl.when(fwd)
        def _(o=o, cs=cs):
            of_scr[pl.ds(row0, ch), cs] = o

        @pl.when(jnp.logical_not(fwd))
        def _(o=o, cs=cs):
            ot = o + of_scr[pl.ds(row0, ch), cs]
            mu = jnp.mean(ot, axis=-1, keepdims=True)
            oc = ot - mu
            var = jnp.mean(oc * oc, axis=-1, keepdims=True)
            on = oc * lax.rsqrt(var + NORM_EPS)
            gate = g_ref[:, cs].astype(F32)
            y_ref[:, cs] = (gate * _sigmoid(gate) * on).astype(BF16)


def _retention(p, lg, dims):
    tt = p.shape[0]
    ch = RET_CHUNK
    b, s, c = dims["batch"], dims["seq"], dims["ctx"]
    ncc, ncl = c // ch, s // ch
    nsteps = ncc + ncl
    ctx_base = (b * s) // ch

    def rblk(bi, ph, t):
        fwd_blk = jnp.where(t < ncc, ctx_base + bi * ncc + t, bi * ncl + (t - ncc))
        bwd_blk = jnp.where(t < ncc, ctx_base + bi * ncc + (ncc - 1 - t), bi * ncl + (ncl - 1 - (t - ncc)))
        return jnp.where(ph == 0, fwd_blk, bwd_blk)

    def oblk(bi, ph, t):
        return rblk(bi, 1, jnp.where(ph == 0, 0, t))

    qk_w = RET_HEADS * RET_DK
    v_w = RET_HEADS * RET_DV
    return pl.pallas_call(
        functools.partial(_ret_kernel, ncc=ncc, ncl=ncl),
        grid=(b, 2, nsteps),
        in_specs=[pl.BlockSpec(memory_space=pltpu.SMEM),
                  pl.BlockSpec((ch, qk_w), lambda bi, ph, t: (rblk(bi, ph, t), COL_RET_Q // qk_w)),
                  pl.BlockSpec((ch, qk_w), lambda bi, ph, t: (rblk(bi, ph, t), COL_RET_K // qk_w)),
                  pl.BlockSpec((ch, v_w), lambda bi, ph, t: (rblk(bi, ph, t), COL_RET_V // v_w)),
                  pl.BlockSpec((ch, v_w), lambda bi, ph, t: (rblk(bi, ph, t), COL_RET_G // v_w))],
        out_specs=pl.BlockSpec((ch, v_w), lambda bi, ph, t: (oblk(bi, ph, t), 0)),
        out_shape=jax.ShapeDtypeStruct((tt, v_w), BF16),
        scratch_shapes=[pltpu.VMEM((RET_HEADS, RET_DK, RET_DV), F32),
                        pltpu.VMEM((nsteps * ch, v_w), F32)],
        compiler_params=_cparams(("arbitrary", "arbitrary", "arbitrary")),
        name="retention",
    )(lg, p, p, p, p)


def _diff_kernel(sc_ref, q_ref, *refs, n_src, tk):
    kv = refs[:2 * n_src]
    g_ref, o_ref = refs[2 * n_src], refs[2 * n_src + 1]
    kx_scr, vx_scr = refs[2 * n_src + 2], refs[2 * n_src + 3]
    streams = (refs[2 * n_src + 4:2 * n_src + 9], refs[2 * n_src + 9:2 * n_src + 14])
    tq = q_ref.shape[0]
    rc = DIFF_ROW_CHUNK
    n_keys = sum(kv[2 * si].shape[0] for si in range(n_src))
    n_tiles = kx_scr.shape[0] // tk

    @pl.when(pl.program_id(2) == 0)
    def _():
        vx_scr[:, LANES:] = jnp.ones((vx_scr.shape[0], LANES), BF16)
        r0 = 0
        for si in range(n_src):
            k_ref, v_ref = kv[2 * si], kv[2 * si + 1]
            kx_scr[r0:r0 + k_ref.shape[0], :] = k_ref[...]
            vx_scr[r0:r0 + v_ref.shape[0], :LANES] = v_ref[...]
            r0 += k_ref.shape[0]
        if r0 < kx_scr.shape[0]:
            kx_scr[r0:, :] = jnp.zeros((kx_scr.shape[0] - r0, LANES), BF16)
            vx_scr[r0:, :LANES] = jnp.zeros((kx_scr.shape[0] - r0, LANES), BF16)

    q = q_ref[...]
    lane = lax.broadcasted_iota(jnp.int32, (1, LANES), 1)
    zero = jnp.zeros_like(q)
    qs = (jnp.where(lane < DIFF_DK, q, zero), jnp.where(lane >= DIFF_DK, q, zero))
    for s0_scr, s1_scr, p_scr, acc_scr, m_scr in streams:
        acc_scr[...] = jnp.zeros_like(acc_scr)
        m_scr[...] = jnp.full(m_scr.shape, NEG_BIG, F32)

    def scores(t, par):
        k0 = t * tk if isinstance(t, int) else pl.multiple_of(t * tk, tk)
        kt = kx_scr[pl.ds(k0, tk), :]
        for st, bufs in enumerate(streams):
            bufs[par][...] = _dot_nt(qs[st], kt)

    def stage(t, par, prefetch, n_valid):
        if prefetch:
            scores(t + 1, 1 - par)
        v0 = t * tk if isinstance(t, int) else pl.multiple_of(t * tk, tk)
        vt = vx_scr[pl.ds(v0, tk), :]
        for bufs in streams:
            s_scr, p_scr, acc_scr, m_scr = bufs[par], bufs[2], bufs[3], bufs[4]
            m_all = m_scr[...]
            m_new, alpha = [], []
            for r in range(tq // rc):
                rows = slice(r * rc, (r + 1) * rc)
                s = s_scr[rows, :]
                if n_valid < tk:
                    col = lax.broadcasted_iota(jnp.int32, s.shape, 1)
                    s = jnp.where(col < n_valid, s, NEG_BIG)
                m_old = m_all[rows]
                mn = jnp.maximum(m_old, jnp.max(s, axis=-1, keepdims=True))
                p_scr[rows, :] = jnp.exp2(s - mn).astype(BF16)
                alpha.append(jnp.exp2(m_old - mn))
                m_new.append(mn)
            m_scr[...] = jnp.concatenate(m_new, axis=0)
            acc_scr[...] = jnp.concatenate(alpha, axis=0) * acc_scr[...] + _dot(p_scr[...], vt)

    scores(0, 0)
    pairs = (n_tiles - 1) // 2
    if pairs > 0:
        def body(u, carry):
            stage(2 * u, 0, True, tk)
            stage(2 * u + 1, 1, True, tk)
            return carry
        lax.fori_loop(0, pairs, body, 0)
    for t in range(2 * pairs, n_tiles):
        last = t == n_tiles - 1
        stage(t, t % 2, not last, n_keys - t * tk if last else tk)
    lam = sc_ref[0]
    a1, a2 = streams[0][3], streams[1][3]
    o = a1[:, :LANES] / a1[:, LANES:] - lam * (a2[:, :LANES] / a2[:, LANES:])
    o = o * lax.rsqrt(jnp.mean(o * o, axis=-1, keepdims=True) + NORM_EPS) * g_ref[...] * sc_ref[1]
    o_ref[...] = o.astype(BF16)


def _diff_attention(p, scal, subln_g, dims, ctx_queries):
    b, s, c = dims["batch"], dims["seq"], dims["ctx"]
    tt = p.shape[0]
    qc0, kc0, vc0 = COL_DIFF_Q // LANES, COL_DIFF_K // LANES, COL_DIFF_V // LANES
    ctx_blk = (b * s) // c
    kctx = pl.BlockSpec((c, LANES), lambda bi, h, qi: (ctx_blk + bi, kc0 + h))
    vctx = pl.BlockSpec((c, LANES), lambda bi, h, qi: (ctx_blk + bi, vc0 + h))
    if ctx_queries:
        tq, nq = c, 1
        q_spec = pl.BlockSpec((tq, LANES), lambda bi, h, qi: (ctx_blk + bi, qc0 + h))
        o_spec = pl.BlockSpec((tq, LANES), lambda bi, h, qi: (bi, h))
        out_rows = b * c
        kv_specs, n_src, kv_args = [kctx, vctx], 1, (p, p)
    else:
        tq = min(DIFF_TQ, s)
        nq = s // tq
        q_spec = pl.BlockSpec((tq, LANES), lambda bi, h, qi: (bi * nq + qi, qc0 + h))
        o_spec = pl.BlockSpec((tq, LANES), lambda bi, h, qi: (bi * nq + qi, h))
        out_rows = b * s
        klat = pl.BlockSpec((s, LANES), lambda bi, h, qi: (bi, kc0 + h))
        vlat = pl.BlockSpec((s, LANES), lambda bi, h, qi: (bi, vc0 + h))
        kv_specs, n_src, kv_args = [klat, vlat, kctx, vctx], 2, (p, p, p, p)
    del tt
    n_keys = c if ctx_queries else s + c
    tk = min(DIFF_TK, n_keys)
    n_pad = -(-n_keys // tk) * tk
    return pl.pallas_call(
        functools.partial(_diff_kernel, n_src=n_src, tk=tk),
        grid=(b, DIFF_HEADS, nq),
        in_specs=[pl.BlockSpec(memory_space=pltpu.SMEM), q_spec] + kv_specs
                 + [pl.BlockSpec((1, LANES), lambda bi, h, qi: (0, 0))],
        out_specs=o_spec,
        out_shape=jax.ShapeDtypeStruct((out_rows, DIFF_HEADS * LANES), BF16),
        scratch_shapes=[pltpu.VMEM((n_pad, LANES), BF16), pltpu.VMEM((n_pad, 2 * LANES), BF16)]
                       + [pltpu.VMEM((tq, tk), F32), pltpu.VMEM((tq, tk), F32), pltpu.VMEM((tq, tk), BF16),
                          pltpu.VMEM((tq, 2 * LANES), F32), pltpu.VMEM((tq, 1), F32)] * 2,
        compiler_params=_cparams(("arbitrary", "arbitrary", "arbitrary")),
        name="diff_ctx" if ctx_queries else "diff_lat",
    )(scal, p, *kv_args, subln_g)


def _roll_bf16(a, shift):
    return pltpu.roll(a.astype(F32), shift, 1).astype(BF16)


def _gqa_core(sink_ref, q_ref, kk, vv, mask, o_ref):
    tq = q_ref.shape[0]
    lane = lax.broadcasted_iota(jnp.int32, (1, LANES), 1)
    half_of_lane = lax.shift_right_logical(lane, 6)
    kk_sw, vv_sw = _roll_bf16(kk, 64), _roll_bf16(vv, 64)
    row = lax.broadcasted_iota(jnp.int32, (2 * tq, 1), 0)
    for kvh in range(GQA_KV_HEADS):
        pairs = (2 * kvh, 2 * kvh + 1)
        outs = []
        for hh in range(2):
            kmat = kk if kvh == hh else kk_sw
            vmat = vv if kvh == hh else vv_sw
            qp = [q_ref[:, p * LANES:(p + 1) * LANES] for p in pairs]
            qs = jnp.concatenate([jnp.where(half_of_lane == hh, x, jnp.zeros_like(x)) for x in qp], axis=0)
            s = _dot_nt(qs, kmat)
            if mask is not None:
                s = jnp.where(mask, s, NEG_BIG)
            sink = jnp.where(row < tq, sink_ref[2 * pairs[0] + hh], sink_ref[2 * pairs[1] + hh])
            m = jnp.maximum(jnp.max(s, axis=-1, keepdims=True), sink)
            pr = jnp.exp(s - m)
            den = jnp.sum(pr, axis=-1, keepdims=True) + jnp.exp(sink - m)
            outs.append(_dot(pr.astype(BF16), vmat) / den)
        for pi, p in enumerate(pairs):
            rs = slice(pi * tq, (pi + 1) * tq)
            o_ref[:, p * LANES:(p + 1) * LANES] = jnp.where(half_of_lane == 0, outs[0][rs], outs[1][rs]).astype(BF16)


def _gqa_lat_kernel(sink_ref, q_ref, kp_ref, ko_ref, kn_ref, kc_ref, vp_ref, vo_ref, vn_ref, vc_ref, o_ref, *, seq):
    tq = q_ref.shape[0]
    qi = pl.program_id(1)
    kk = jnp.concatenate([kp_ref[...], ko_ref[...], kn_ref[...], kc_ref[...]], axis=0)
    vv = jnp.concatenate([vp_ref[...], vo_ref[...], vn_ref[...], vc_ref[...]], axis=0)
    nk = kk.shape[0]
    n_loc = tq + 2 * WINDOW
    r2 = lax.broadcasted_iota(jnp.int32, (2 * tq, nk), 0)
    r = jnp.where(r2 >= tq, r2 - tq, r2)
    c = lax.broadcasted_iota(jnp.int32, (2 * tq, nk), 1)
    kpos = qi * tq - WINDOW + c
    mask = ((jnp.abs(r - c + WINDOW) <= WINDOW) & (kpos >= 0) & (kpos < seq)) | (c >= n_loc)
    _gqa_core(sink_ref, q_ref, kk, vv, mask, o_ref)


def _gqa_ctx_kernel(sink_ref, q_ref, kc_ref, vc_ref, o_ref):
    _gqa_core(sink_ref, q_ref, kc_ref[...], vc_ref[...], None, o_ref)


def _gqa(p, sink, dims, ctx_queries):
    b, s, c = dims["batch"], dims["seq"], dims["ctx"]
    qw = GQA_HEADS * GQA_DH
    qc0, kc0, vc0 = COL_GQA_Q // qw, COL_GQA_K // LANES, COL_GQA_V // LANES
    ctx_blk = (b * s) // c
    sm = pl.BlockSpec(memory_space=pltpu.SMEM)
    if ctx_queries:
        return pl.pallas_call(
            _gqa_ctx_kernel,
            grid=(b,),
            in_specs=[sm, pl.BlockSpec((c, qw), lambda bi: (ctx_blk + bi, qc0)),
                      pl.BlockSpec((c, LANES), lambda bi: (ctx_blk + bi, kc0)),
                      pl.BlockSpec((c, LANES), lambda bi: (ctx_blk + bi, vc0))],
            out_specs=pl.BlockSpec((c, qw), lambda bi: (bi, 0)),
            out_shape=jax.ShapeDtypeStruct((b * c, qw), BF16),
            compiler_params=_cparams(("arbitrary",)),
            name="gqa_ctx",
        )(sink, p, p, p)
    tq = GQA_TQ
    nq = s // tq
    wpq = tq // WINDOW
    nwb = s // WINDOW

    def loc_specs(col):
        return [pl.BlockSpec((WINDOW, LANES), lambda bi, qi: (bi * nwb + jnp.maximum(qi * wpq - 1, 0), col)),
                pl.BlockSpec((tq, LANES), lambda bi, qi: (bi * nq + qi, col)),
                pl.BlockSpec((WINDOW, LANES), lambda bi, qi: (bi * nwb + jnp.minimum(qi * wpq + wpq, nwb - 1), col)),
                pl.BlockSpec((c, LANES), lambda bi, qi: (ctx_blk + bi, col))]

    return pl.pallas_call(
        functools.partial(_gqa_lat_kernel, seq=s),
        grid=(b, nq),
        in_specs=[sm, pl.BlockSpec((tq, qw), lambda bi, qi: (bi * nq + qi, qc0))] + loc_specs(kc0) + loc_specs(vc0),
        out_specs=pl.BlockSpec((tq, qw), lambda bi, qi: (bi * nq + qi, 0)),
        out_shape=jax.ShapeDtypeStruct((b * s, qw), BF16),
        compiler_params=_cparams(("arbitrary", "arbitrary")),
        name="gqa_lat",
    )(sink, p, *([p] * 8))


def _merge_kernel(x_ref, yr_ref, yd_ref, yg_ref, gmix_ref, gffn_ref, sh1_ref, sc1_ref, ga1_ref, sh2_ref, sc2_ref,
                  wg_ref, wb_ref, wo_ref, wrh_ref, wrl_ref, br_ref, x1_ref, h2_ref, ti_ref, tw_ref):
    d = x_ref.shape[1]
    x = x_ref[...]
    h = (_rms(x, gmix_ref[...]) * (1.0 + sc1_ref[0]) + sh1_ref[0]).astype(BF16)
    rw = RET_HEADS * RET_DV
    dw = DIFF_HEADS * LANES
    merged = None
    for bi, (y_ref, r0, r1) in enumerate(((yr_ref, 0, rw), (yd_ref, rw, rw + dw), (yg_ref, rw + dw, wb_ref.shape[0]))):
        gate = _sigmoid(_dot(h, wg_ref[:, bi * d:(bi + 1) * d]))
        term = gate * _dot(y_ref[...], wb_ref[r0:r1, :])
        merged = term if merged is None else merged + term
    x1 = x + ga1_ref[0] * _dot(merged.astype(BF16), wo_ref[...])
    x1_ref[...] = x1
    h2 = _rms(x1, gffn_ref[...]) * (1.0 + sc2_ref[0]) + sh2_ref[0]
    h2_ref[...] = h2
    hh, hl = _split_bf16(h2)
    wrh, wrl = wrh_ref[...], wrl_ref[...]
    logits = _dot_nt(wrh, hh) + (_dot_nt(wrl, hh) + _dot_nt(wrh, hl)) + br_ref[...]
    eidx = lax.broadcasted_iota(jnp.int32, logits.shape, 0).astype(F32)
    vals = []
    for k in range(TOP_K):
        mx = jnp.max(logits, axis=0, keepdims=True)
        ix = jnp.min(jnp.where(logits == mx, eidx, float(N_EXPERTS)), axis=0, keepdims=True)
        vals.append(mx)
        ti_ref[k:k + 1, :] = ix.astype(jnp.int32)
        logits = jnp.where(eidx == ix, -jnp.inf, logits)
    es = [jnp.exp(v - vals[0]) for v in vals]
    tot = es[0] + es[1] + es[2] + es[3]
    for k in range(TOP_K):
        tw_ref[k:k + 1, :] = es[k] / tot


def _merge(xall, yr, yd, yg, gmix, gffn, mod3, wg, wb, wo, wrh, wrl, br, dims, ntok):
    d = xall.shape[1]
    tm = dims["tm_merge"]
    tps = dims["seq"] // tm

    def mrow(i):
        return jnp.minimum(i // tps, dims["batch"])

    def modspec(k):
        return pl.BlockSpec((1, 1, d), lambda i: (mrow(i), 0, k))

    def full(a):
        return pl.BlockSpec(a.shape, lambda i: (0,) * a.ndim)

    def rows(w):
        return pl.BlockSpec((tm, w), lambda i: (i, 0))

    return pl.pallas_call(
        _merge_kernel,
        grid=(ntok // tm,),
        in_specs=[rows(d), rows(yr.shape[1]), rows(yd.shape[1]), rows(yg.shape[1]), full(gmix), full(gffn),
                  modspec(0), modspec(1), modspec(2), modspec(3), modspec(4),
                  full(wg), full(wb), full(wo), full(wrh), full(wrl), full(br)],
        out_specs=[rows(d), rows(d), pl.BlockSpec((TOP_K, tm), lambda i: (0, i)),
                   pl.BlockSpec((TOP_K, tm), lambda i: (0, i))],
        out_shape=[jax.ShapeDtypeStruct((ntok, d), F32), jax.ShapeDtypeStruct((ntok, d), F32),
                   jax.ShapeDtypeStruct((TOP_K, ntok), jnp.int32), jax.ShapeDtypeStruct((TOP_K, ntok), F32)],
        compiler_params=_cparams(("arbitrary",)),
        name="merge",
    )(xall, yr, yd, yg, gmix, gffn, mod3, mod3, mod3, mod3, mod3, wg, wb, wo, wrh, wrl, br)


TOK_BITS = 15
FFN_CHUNK = 256


def _moe_kernel(be_ref, nv_ref, nu_ref, code_ref, sw_ref, h2_hbm, wgu_ref, bgu_ref, wdn_ref, bdn_ref, o_hbm,
                xbuf, ybuf, wgu_bf, wdn_bf, gsem, ssem):
    i = pl.program_id(0)
    nu = nu_ref[0]
    slot = i % 2
    nb = MOE_BLOCK
    d = wdn_bf.shape[0]

    def gather_rows(blk, sl, r0, r1):
        for r in range(r0, r1):
            tok = code_ref[blk * nb + r] & ((1 << TOK_BITS) - 1)
            pltpu.make_async_copy(h2_hbm.at[pl.ds(tok, 1)], xbuf.at[sl, pl.ds(r, 1)], gsem.at[sl]).start()

    def for_rows(n, body):
        @pl.when(n == nb)
        def _():
            for r in range(nb):
                body(r)

        @pl.when(n != nb)
        def _():
            def step(r, carry):
                body(r)
                return carry
            lax.fori_loop(0, n, step, 0)

    def wait_rows(n, copy_of_rows):
        @pl.when(n == nb)
        def _():
            copy_of_rows(nb).wait()

        @pl.when(n != nb)
        def _():
            def body(r, carry):
                copy_of_rows(1).wait()
                return carry
            lax.fori_loop(0, n, body, 0)

    def gather_wait(sl):
        pltpu.make_async_copy(h2_hbm.at[pl.ds(0, nb)], xbuf.at[sl], gsem.at[sl]).wait()

    def scatter_start(blk, sl):
        def body(r):
            dst = lax.shift_right_logical(code_ref[blk * nb + r], TOK_BITS)
            pltpu.make_async_copy(ybuf.at[sl, pl.ds(r, 1)], o_hbm.at[pl.ds(dst, 1)], ssem.at[sl]).start()
        for_rows(nv_ref[blk], body)

    def scatter_wait(blk, sl):
        wait_rows(nv_ref[blk], lambda n: pltpu.make_async_copy(
            ybuf.at[sl, pl.ds(0, n)], o_hbm.at[pl.ds(0, n)], ssem.at[sl]))

    @pl.when(i < nu)
    def _():
        @pl.when(i == 0)
        def _():
            gather_rows(0, 0, 0, nb)

        @pl.when((i == 0) | (be_ref[i] != be_ref[jnp.maximum(i - 1, 0)]))
        def _():
            rc = 128

            def cast(r, carry):
                r0 = pl.multiple_of(r * rc, rc)
                wgu_bf[pl.ds(r0, rc), :] = wgu_ref[0, 0, pl.ds(r0, rc), :].astype(BF16)
                wdn_bf[pl.ds(r0, rc), :] = wdn_ref[0, 0, pl.ds(r0, rc), :].astype(BF16)
                return carry
            lax.fori_loop(0, d // rc, cast, 0)

        gather_wait(slot)
        nxt = jnp.minimum(i + 1, pl.num_programs(0) - 1)
        x = xbuf[slot].astype(BF16)
        cw = FFN_CHUNK
        nch = d // cw
        y = None
        for ci in range(nch):
            c0, c1 = ci * cw, (ci + 1) * cw
            glu = jnp.minimum(_dot(x, wgu_bf[:, c0:c1]) + bgu_ref[0, 0, :, c0:c1], SWIGLU_LIMIT)
            lin = jnp.clip(_dot(x, wgu_bf[:, d + c0:d + c1]) + bgu_ref[0, 0, :, d + c0:d + c1],
                           -SWIGLU_LIMIT, SWIGLU_LIMIT)
            act = glu * _sigmoid(SWIGLU_ALPHA * glu) * (lin + 1.0)
            part = _dot(act.astype(BF16), wdn_bf[c0:c1, :])
            y = part if y is None else y + part
            gather_rows(nxt, 1 - slot, ci * (nb // nch), (ci + 1) * (nb // nch))
        y = (y + bdn_ref[0, 0]) * sw_ref[0]

        @pl.when(i >= 2)
        def _():
            scatter_wait(i - 2, slot)

        ybuf[slot] = y
        scatter_start(i, slot)

        @pl.when(i == nu - 1)
        def _():
            gather_wait(1 - slot)

            @pl.when(i >= 1)
            def _():
                scatter_wait(i - 1, 1 - slot)
            scatter_wait(i, slot)


def _moe(h2, route, w_gu, b_gu, w_dn, b_dn, layer):
    codes, slot_w, block_expert, n_valid, n_used = route
    ntok, d = h2.shape
    n_blocks = block_expert.shape[0]
    depth, ne = w_gu.shape[:2]

    def wspec(rows, cols):
        return pl.BlockSpec((1, 1, rows, cols), lambda i, be, nv, nu, cd: (layer, be[i], 0, 0))

    grid_spec = pltpu.PrefetchScalarGridSpec(
        num_scalar_prefetch=4,
        grid=(n_blocks,),
        in_specs=[pl.BlockSpec((1, MOE_BLOCK, 1), lambda i, be, nv, nu, cd: (i, 0, 0)),
                  pl.BlockSpec(memory_space=pl.ANY),
                  wspec(d, 2 * d), wspec(1, 2 * d), wspec(d, d), wspec(1, d)],
        out_specs=pl.BlockSpec(memory_space=pl.ANY),
        scratch_shapes=[pltpu.VMEM((2, MOE_BLOCK, d), F32), pltpu.VMEM((2, MOE_BLOCK, d), F32),
                        pltpu.VMEM((d, 2 * d), BF16), pltpu.VMEM((d, d), BF16),
                        pltpu.SemaphoreType.DMA((2,)), pltpu.SemaphoreType.DMA((2,))])
    return pl.pallas_call(
        _moe_kernel,
        grid_spec=grid_spec,
        out_shape=jax.ShapeDtypeStruct((TOP_K * ntok, d), F32),
        compiler_params=_cparams(("arbitrary",)),
        name="moe",
    )(block_expert, n_valid, n_used, codes, slot_w, h2, w_gu, b_gu.reshape(depth, ne, 1, 2 * d), w_dn,
      b_dn.reshape(depth, ne, 1, d))


def _route(ti, tw, ntok):
    a = TOP_K * ntok
    n_blocks = (a + N_EXPERTS * (MOE_BLOCK - 1) + MOE_BLOCK - 1) // MOE_BLOCK
    expert = ti.reshape(a)
    wts = tw.reshape(a)
    order = jnp.argsort(expert).astype(jnp.int32)
    counts = jnp.sum((expert[:, None] == jnp.arange(N_EXPERTS, dtype=jnp.int32)[None, :]).astype(jnp.int32), axis=0)
    padded = (counts + MOE_BLOCK - 1) // MOE_BLOCK * MOE_BLOCK
    pad_end = jnp.cumsum(padded)
    pad_start = pad_end - padded
    cstart = jnp.cumsum(counts) - counts
    n_used = (pad_end[-1:] // MOE_BLOCK).astype(jnp.int32)
    block_start = jnp.arange(n_blocks, dtype=jnp.int32) * MOE_BLOCK
    block_expert = jnp.minimum(jnp.sum((pad_end[None, :] <= block_start[:, None]).astype(jnp.int32), axis=1),
                               N_EXPERTS - 1)
    n_valid = jnp.clip(counts[block_expert] - (block_start - pad_start[block_expert]), 0, MOE_BLOCK).astype(jnp.int32)
    e_s = jnp.repeat(block_expert, MOE_BLOCK)
    slot = jnp.arange(n_blocks * MOE_BLOCK, dtype=jnp.int32)
    r = slot - pad_start[e_s]
    valid = r < counts[e_s]
    src = jnp.where(valid, order[jnp.clip(cstart[e_s] + r, 0, a - 1)], 0)
    codes = lax.shift_left(src, TOK_BITS) | (src % ntok)
    slot_w = jnp.where(valid, wts[src], 0.0).reshape(n_blocks, MOE_BLOCK, 1)
    return codes, slot_w, block_expert, n_valid, n_used


def _combine_kernel(x1_ref, o0_ref, o1_ref, o2_ref, o3_ref, ga2_ref, gf_ref, x2_ref, *, final):
    y = (o0_ref[...] + o1_ref[...]) + (o2_ref[...] + o3_ref[...])
    x2 = x1_ref[...] + ga2_ref[0] * y
    x2_ref[...] = _rms(x2, gf_ref[...]) if final else x2


def _combine(x1, o4, mod3, g_final, dims, final):
    ntok, d = x1.shape
    tm = dims["tm_merge"]
    tps = dims["seq"] // tm
    nt = ntok // tm

    def ospec(k):
        return pl.BlockSpec((tm, d), lambda i: (k * nt + i, 0))

    return pl.pallas_call(
        functools.partial(_combine_kernel, final=final),
        grid=(nt,),
        in_specs=[pl.BlockSpec((tm, d), lambda i: (i, 0)), ospec(0), ospec(1), ospec(2), ospec(3),
                  pl.BlockSpec((1, 1, d), lambda i: (jnp.minimum(i // tps, dims["batch"]), 0, 5)),
                  pl.BlockSpec((1, d), lambda i: (0, 0))],
        out_specs=pl.BlockSpec((tm, d), lambda i: (i, 0)),
        out_shape=jax.ShapeDtypeStruct((ntok, d), F32),
        compiler_params=_cparams(("arbitrary",)),
        name="combine",
    )(x1, o4, o4, o4, o4, mod3, g_final)


def _rope_tables(seq, pad_rows):
    pos = jnp.arange(seq, dtype=jnp.int32)
    row = (pos // GRID_W).astype(F32)
    col = (pos % GRID_W).astype(F32)

    def cs(head_dim):
        quarter = head_dim // 4
        inv_freq = ROPE_BASE ** (-jnp.arange(quarter, dtype=F32) / quarter)
        ang = jnp.concatenate([row[:, None] * inv_freq, col[:, None] * inv_freq], axis=-1)
        return jnp.cos(ang), jnp.sin(ang)

    c128, s128 = cs(RET_DK)
    c64, s64 = cs(DIFF_DK)
    ones = jnp.ones((pad_rows, LANES), F32)
    zeros = jnp.zeros((pad_rows, LANES), F32)
    a128 = jnp.concatenate([jnp.concatenate([c128, c128], -1), ones], 0)
    b128 = jnp.concatenate([jnp.concatenate([-s128, s128], -1), zeros], 0)
    a64 = jnp.concatenate([jnp.concatenate([c64, c64, c64, c64], -1), ones], 0)
    b64 = jnp.concatenate([jnp.concatenate([-s64, s64, -s64, s64], -1), zeros], 0)
    return a128, b128, a64, b64


def _pick_tile(cands, *sizes):
    for t in cands:
        if all(s % t == 0 for s in sizes):
            return t
    raise ValueError("no tile size fits")


def kernel(x, c, ctx, c_ctx, w_mod, b_mod, g_norm_mix, g_norm_ffn, w_in, ret_decay_fwd, ret_decay_bwd, diff_lambda_q1, diff_lambda_k1, diff_lambda_q2, diff_lambda_k2, diff_subln_g, gqa_sink, w_branch, w_out, w_router, b_router, w_gate_up, b_gate_up, w_down, b_down, g_final):
    b, s, d = x.shape
    n_ctx = ctx.shape[1]
    depth = w_mod.shape[0]
    n_lat = b * s
    tt = n_lat + b * n_ctx
    assert s % RET_CHUNK == 0 and n_ctx % RET_CHUNK == 0 and s % GQA_TQ == 0 and n_lat % n_ctx == 0
    assert tt < (1 << TOK_BITS) and b < 8
    tm = _pick_tile((1024, 512, 256), s, b * n_ctx)
    tm_merge = _pick_tile((512, 256), s, b * n_ctx)
    dims = dict(batch=b, seq=s, ctx=n_ctx, tm=tm, tm_merge=tm_merge,
                n_lat_tiles=n_lat // tm, tiles_per_seq=s // tm)

    xall = jnp.concatenate([x.reshape(n_lat, d), ctx.reshape(b * n_ctx, d)], axis=0)
    cvec = jnp.concatenate([c, c_ctx[None, :], jnp.zeros((8 - b - 1, d), F32)], axis=0)
    mod = _mod_all(cvec, w_mod, b_mod)
    tabs = _rope_tables(s, tm)

    for l in range(depth):
        last = l == depth - 1
        mod3 = mod[l].reshape(8, 1, 6 * d)
        w_l = w_in[l]
        p = _inproj(xall, g_norm_mix[l][None, :], mod3, w_l[:, :P_COLS].astype(BF16), tabs, dims)

        lg = jnp.stack([jax.nn.log_sigmoid(ret_decay_fwd[l].astype(F32)),
                        jax.nn.log_sigmoid(ret_decay_bwd[l].astype(F32))], axis=0)
        y_r = _retention(p, lg, dims)

        lam_init = 0.8 - 0.6 * math.exp(-0.3 * l)
        lam = (jnp.exp(jnp.sum(diff_lambda_q1[l].astype(F32) * diff_lambda_k1[l].astype(F32)))
               - jnp.exp(jnp.sum(diff_lambda_q2[l].astype(F32) * diff_lambda_k2[l].astype(F32))) + lam_init)
        scal = jnp.stack([lam, jnp.asarray(1.0 - lam_init, F32)])
        subln = diff_subln_g[l][None, :].astype(F32)
        sink = gqa_sink[l].astype(F32)
        y_d = _diff_attention(p, scal, subln, dims, False)
        y_g = _gqa(p, sink, dims, False)
        if not last:
            y_d = jnp.concatenate([y_d, _diff_attention(p, scal, subln, dims, True)], axis=0)
            y_g = jnp.concatenate([y_g, _gqa(p, sink, dims, True)], axis=0)
        ntok = n_lat if last else tt

        wr = w_router[l].T
        wrh, wrl = _split_bf16(wr)
        x1, h2, ti, tw = _merge(xall, y_r, y_d, y_g, g_norm_mix[l][None, :], g_norm_ffn[l][None, :], mod3,
                                w_l[:, P_COLS:].astype(BF16), w_branch[l].astype(BF16), w_out[l].astype(BF16),
                                wrh, wrl, b_router[l][:, None].astype(F32), dims, ntok)
        o4 = _moe(h2, _route(ti, tw, ntok), w_gate_up, b_gate_up, w_down, b_down, l)
        xall = _combine(x1, o4, mod3, g_final[None, :], dims, last)
    return xall.reshape(b, s, d)
```

```python
import functools
import math

import jax
import jax.numpy as jnp
from jax import lax
from jax.experimental import pallas as pl
from jax.experimental.pallas import tpu as pltpu

F32 = jnp.float32
BF16 = jnp.bfloat16

GRID_W = 64
RET_HEADS, RET_DK, RET_DV = 4, 128, 256
DIFF_HEADS, DIFF_DK = 4, 64
GQA_HEADS, GQA_KV_HEADS, GQA_DH = 8, 2, 64
WINDOW = 128
N_EXPERTS, TOP_K = 32, 4
SWIGLU_ALPHA, SWIGLU_LIMIT = 1.702, 7.0
MOE_BLOCK = 256
ROPE_BASE = 10000.0
NORM_EPS = 1e-6
NEG_BIG = -1e30

LANES = 128
P_COLS = 5376
IN_TN = 768
RET_CHUNK = 256
DIFF_TQ, DIFF_TK = 512, 512
DIFF_ROW_CHUNK = 64
LOG2E = math.log2(math.e)
GQA_TQ = 256
VMEM_LIMIT = 56 * 1024 * 1024

_PLAIN = ("plain", 1.0)
_GROUP_KINDS = ([("r128", RET_DK ** -0.5)] * 4 + [("r128", 1.0)] * 4 + [_PLAIN] * 16
                + [("r64", DIFF_DK ** -0.5 * LOG2E)] * 4 + [("r64", 1.0)] * 4 + [_PLAIN] * 4
                + [("r64", GQA_DH ** -0.5)] * 4 + [("r64", 1.0)] + [_PLAIN])
COL_RET_Q, COL_RET_K, COL_RET_V, COL_RET_G = 0, 512, 1024, 2048
COL_DIFF_Q, COL_DIFF_K, COL_DIFF_V = 3072, 3584, 4096
COL_GQA_Q, COL_GQA_K, COL_GQA_V = 4608, 5120, 5248


def _cparams(sem):
    return pltpu.CompilerParams(dimension_semantics=sem, vmem_limit_bytes=VMEM_LIMIT)


def _sigmoid(x):
    return 1.0 / (1.0 + jnp.exp(-x))


def _dot(a, b):
    return jnp.dot(a, b, preferred_element_type=F32)


def _dot_nt(a, b):
    return lax.dot_general(a, b, (((1,), (1,)), ((), ())), preferred_element_type=F32)


def _dot_tn(a, b):
    return lax.dot_general(a, b, (((0,), (0,)), ((), ())), preferred_element_type=F32)


def _split_bf16(a):
    hi = a.astype(BF16)
    lo = (a - hi.astype(F32)).astype(BF16)
    return hi, lo


def _rms(x, g):
    return x * lax.rsqrt(jnp.mean(x * x, axis=-1, keepdims=True) + NORM_EPS) * g


def _mod_kernel(c_ref, w_ref, b_ref, o_ref):
    c = c_ref[...]
    s = c * _sigmoid(c)
    sh, sl = _split_bf16(s)
    wh, wl = _split_bf16(w_ref[0])
    o_ref[0] = _dot(sh, wh) + (_dot(sl, wh) + _dot(sh, wl)) + b_ref[0]


def _mod_all(cvec, w_mod, b_mod):
    depth, d, n = w_mod.shape
    tn = n // 4
    return pl.pallas_call(
        _mod_kernel,
        grid=(depth, n // tn),
        in_specs=[pl.BlockSpec((8, d), lambda l, j: (0, 0)),
                  pl.BlockSpec((1, d, tn), lambda l, j: (l, 0, j)),
                  pl.BlockSpec((1, 1, tn), lambda l, j: (l, 0, j))],
        out_specs=pl.BlockSpec((1, 8, tn), lambda l, j: (l, 0, j)),
        out_shape=jax.ShapeDtypeStruct((depth, 8, n), F32),
        compiler_params=_cparams(("arbitrary", "arbitrary")),
        name="mod",
    )(cvec, w_mod, b_mod.reshape(depth, 1, n))


def _rope_group(a, kind, scale, a128, b128, a64, b64):
    if kind == "plain":
        return a
    if kind == "r128":
        out = a * a128 + pltpu.roll(a, 64, 1) * b128
    else:
        lane = lax.broadcasted_iota(jnp.int32, (1, LANES), 1)
        first = (lane & 63) < 32
        partner = jnp.where(first, pltpu.roll(a, 96, 1), pltpu.roll(a, 32, 1))
        out = a * a64 + partner * b64
    return out * scale if scale != 1.0 else out


def _inproj_kernel(x_ref, g_ref, sc_ref, sh_ref, w_ref, a128_ref, b128_ref, a64_ref, b64_ref,
                   o_ref, h_scr, acc_scr, *, n_tiles):
    j = pl.program_id(1)

    @pl.when(j == 0)
    def _():
        y = _rms(x_ref[...], g_ref[...])
        h_scr[...] = (y * (1.0 + sc_ref[0]) + sh_ref[0]).astype(BF16)

    acc_scr[...] = _dot(h_scr[...], w_ref[...])
    gpt = IN_TN // LANES
    for jj in range(n_tiles):
        @pl.when(j == jj)
        def _(jj=jj):
            for g in range(gpt):
                kind, scale = _GROUP_KINDS[jj * gpt + g]
                a = acc_scr[:, g * LANES:(g + 1) * LANES]
                out = _rope_group(a, kind, scale, a128_ref[...], b128_ref[...], a64_ref[...], b64_ref[...])
                o_ref[:, g * LANES:(g + 1) * LANES] = out.astype(BF16)


def _inproj(xall, g, mod3, w_bf, tabs, dims):
    tt, d = xall.shape
    tm = dims["tm"]
    n_lat_tiles, tps = dims["n_lat_tiles"], dims["tiles_per_seq"]
    n_tiles = P_COLS // IN_TN

    def mrow(i):
        return jnp.minimum(i // tps, dims["batch"])

    def trow(i):
        return jnp.where(i < n_lat_tiles, i % tps, tps)

    tab_spec = pl.BlockSpec((tm, LANES), lambda i, j: (trow(i), 0))
    return pl.pallas_call(
        functools.partial(_inproj_kernel, n_tiles=n_tiles),
        grid=(tt // tm, n_tiles),
        in_specs=[pl.BlockSpec((tm, d), lambda i, j: (i, 0)),
                  pl.BlockSpec((1, d), lambda i, j: (0, 0)),
                  pl.BlockSpec((1, 1, d), lambda i, j: (mrow(i), 0, 1)),
                  pl.BlockSpec((1, 1, d), lambda i, j: (mrow(i), 0, 0)),
                  pl.BlockSpec((d, IN_TN), lambda i, j: (0, j)),
                  tab_spec, tab_spec, tab_spec, tab_spec],
        out_specs=pl.BlockSpec((tm, IN_TN), lambda i, j: (i, j)),
        out_shape=jax.ShapeDtypeStruct((tt, P_COLS), BF16),
        scratch_shapes=[pltpu.VMEM((tm, d), BF16), pltpu.VMEM((tm, IN_TN), F32)],
        compiler_params=_cparams(("arbitrary", "arbitrary")),
        name="inproj",
    )(xall, g, mod3, mod3, w_bf, *tabs)


def _ret_kernel(lg_ref, q_ref, k_ref, v_ref, g_ref, y_ref, s_scr, of_scr, *, ncc, ncl):
    ph = pl.program_id(1)
    t = pl.program_id(2)
    ch = RET_CHUNK

    @pl.when(t == 0)
    def _():
        s_scr[...] = jnp.zeros_like(s_scr)

    fwd = ph == 0
    ii = lax.broadcasted_iota(jnp.int32, (ch, ch), 0).astype(F32)
    jj = lax.broadcasted_iota(jnp.int32, (ch, ch), 1).astype(F32)
    rel = jnp.where(fwd, ii - jj, jj - ii)
    pos = lax.broadcasted_iota(jnp.int32, (ch, 1), 0).astype(F32)
    qpos = jnp.where(fwd, pos + 1.0, ch - pos)
    kpos = jnp.where(fwd, ch - 1.0 - pos, pos)
    mirror = jnp.where(t < ncc, ncc - 1 - t, ncc + ncl - 1 - (t - ncc))
    row0 = pl.multiple_of(jnp.where(fwd, t, mirror) * ch, ch)

    for h in range(RET_HEADS):
        lg = lg_ref[ph, h]
        dmask = jnp.where(rel >= 0.0, jnp.exp(jnp.maximum(rel, 0.0) * lg), 0.0)
        q = q_ref[:, h * RET_DK:(h + 1) * RET_DK]
        k = k_ref[:, h * RET_DK:(h + 1) * RET_DK]
        v = v_ref[:, h * RET_DV:(h + 1) * RET_DV]
        a = (_dot_nt(q, k) * dmask).astype(BF16)
        qd = (q.astype(F32) * jnp.exp(qpos * lg)).astype(BF16)
        kd = (k.astype(F32) * jnp.exp(kpos * lg)).astype(BF16)
        state = s_scr[h]
        o = _dot(a, v) + _dot(qd, state.astype(BF16))
        s_scr[h] = state * jnp.exp(ch * lg) + _dot_tn(kd, v)
        cs = slice(h * RET_DV, (h + 1) * RET_DV)

        @pl.when(fwd)
        def _(o=o, cs=cs):
            of_scr[pl.ds(row0, ch), cs] = o

        @pl.when(jnp.logical_not(fwd))
        def _(o=o, cs=cs):
            ot = o + of_scr[pl.ds(row0, ch), cs]
            mu = jnp.mean(ot, axis=-1, keepdims=True)
            oc = ot - mu
            var = jnp.mean(oc * oc, axis=-1, keepdims=True)
            on = oc * lax.rsqrt(var + NORM_EPS)
            gate = g_ref[:, cs].astype(F32)
            y_ref[:, cs] = (gate * _sigmoid(gate) * on).astype(BF16)


def _retention(p, lg, dims):
    tt = p.shape[0]
    ch = RET_CHUNK
    b, s, c = dims["batch"], dims["seq"], dims["ctx"]
    ncc, ncl = c // ch, s // ch
    nsteps = ncc + ncl
    ctx_base = (b * s) // ch

    def rblk(bi, ph, t):
        fwd_blk = jnp.where(t < ncc, ctx_base + bi * ncc + t, bi * ncl + (t - ncc))
        bwd_blk = jnp.where(t < ncc, ctx_base + bi * ncc + (ncc - 1 - t), bi * ncl + (ncl - 1 - (t - ncc)))
        return jnp.where(ph == 0, fwd_blk, bwd_blk)

    def oblk(bi, ph, t):
        return rblk(bi, 1, jnp.where(ph == 0, 0, t))

    qk_w = RET_HEADS * RET_DK
    v_w = RET_HEADS * RET_DV
    return pl.pallas_call(
        functools.partial(_ret_kernel, ncc=ncc, ncl=ncl),
        grid=(b, 2, nsteps),
        in_specs=[pl.BlockSpec(memory_space=pltpu.SMEM),
                  pl.BlockSpec((ch, qk_w), lambda bi, ph, t: (rblk(bi, ph, t), COL_RET_Q // qk_w)),
                  pl.BlockSpec((ch, qk_w), lambda bi, ph, t: (rblk(bi, ph, t), COL_RET_K // qk_w)),
                  pl.BlockSpec((ch, v_w), lambda bi, ph, t: (rblk(bi, ph, t), COL_RET_V // v_w)),
                  pl.BlockSpec((ch, v_w), lambda bi, ph, t: (rblk(bi, ph, t), COL_RET_G // v_w))],
        out_specs=pl.BlockSpec((ch, v_w), lambda bi, ph, t: (oblk(bi, ph, t), 0)),
        out_shape=jax.ShapeDtypeStruct((tt, v_w), BF16),
        scratch_shapes=[pltpu.VMEM((RET_HEADS, RET_DK, RET_DV), F32),
                        pltpu.VMEM((nsteps * ch, v_w), F32)],
        compiler_params=_cparams(("arbitrary", "arbitrary", "arbitrary")),
        name="retention",
    )(lg, p, p, p, p)


def _diff_kernel(sc_ref, q_ref, *refs, n_src, tk):
    kv = refs[:2 * n_src]
    g_ref, o_ref = refs[2 * n_src], refs[2 * n_src + 1]
    kx_scr, vx_scr = refs[2 * n_src + 2], refs[2 * n_src + 3]
    streams = (refs[2 * n_src + 4:2 * n_src + 9], refs[2 * n_src + 9:2 * n_src + 14])
    tq = q_ref.shape[0]
    rc = DIFF_ROW_CHUNK
    n_keys = sum(kv[2 * si].shape[0] for si in range(n_src))
    n_tiles = kx_scr.shape[0] // tk

    @pl.when(pl.program_id(2) == 0)
    def _():
        vx_scr[:, LANES:] = jnp.ones((vx_scr.shape[0], LANES), BF16)
        r0 = 0
        for si in range(n_src):
            k_ref, v_ref = kv[2 * si], kv[2 * si + 1]
            kx_scr[r0:r0 + k_ref.shape[0], :] = k_ref[...]
            vx_scr[r0:r0 + v_ref.shape[0], :LANES] = v_ref[...]
            r0 += k_ref.shape[0]
        if r0 < kx_scr.shape[0]:
            kx_scr[r0:, :] = jnp.zeros((kx_scr.shape[0] - r0, LANES), BF16)
            vx_scr[r0:, :LANES] = jnp.zeros((kx_scr.shape[0] - r0, LANES), BF16)

    q = q_ref[...]
    lane = lax.broadcasted_iota(jnp.int32, (1, LANES), 1)
    zero = jnp.zeros_like(q)
    qs = (jnp.where(lane < DIFF_DK, q, zero), jnp.where(lane >= DIFF_DK, q, zero))
    for s0_scr, s1_scr, p_scr, acc_scr, m_scr in streams:
        acc_scr[...] = jnp.zeros_like(acc_scr)
        m_scr[...] = jnp.full(m_scr.shape, NEG_BIG, F32)

    def scores(t, par):
        k0 = t * tk if isinstance(t, int) else pl.multiple_of(t * tk, tk)
        kt = kx_scr[pl.ds(k0, tk), :]
        for st, bufs in enumerate(streams):
            bufs[par][...] = _dot_nt(qs[st], kt)

    def stage(t, par, prefetch, n_valid):
        if prefetch:
            scores(t + 1, 1 - par)
        v0 = t * tk if isinstance(t, int) else pl.multiple_of(t * tk, tk)
        vt = vx_scr[pl.ds(v0, tk), :]
        for bufs in streams:
            s_scr, p_scr, acc_scr, m_scr = bufs[par], bufs[2], bufs[3], bufs[4]
            m_all = m_scr[...]
            m_new, alpha = [], []
            for r in range(tq // rc):
                rows = slice(r * rc, (r + 1) * rc)
                s = s_scr[rows, :]
                if n_valid < tk:
                    col = lax.broadcasted_iota(jnp.int32, s.shape, 1)
                    s = jnp.where(col < n_valid, s, NEG_BIG)
                m_old = m_all[rows]
                mn = jnp.maximum(m_old, jnp.max(s, axis=-1, keepdims=True))
                p_scr[rows, :] = jnp.exp2(s - mn).astype(BF16)
                alpha.append(jnp.exp2(m_old - mn))
                m_new.append(mn)
            m_scr[...] = jnp.concatenate(m_new, axis=0)
            acc_scr[...] = jnp.concatenate(alpha, axis=0) * acc_scr[...] + _dot(p_scr[...], vt)

    scores(0, 0)
    pairs = (n_tiles - 1) // 2
    if pairs > 0:
        def body(u, carry):
            stage(2 * u, 0, True, tk)
            stage(2 * u + 1, 1, True, tk)
            return carry
        lax.fori_loop(0, pairs, body, 0)
    for t in range(2 * pairs, n_tiles):
        last = t == n_tiles - 1
        stage(t, t % 2, not last, n_keys - t * tk if last else tk)
    lam = sc_ref[0]
    a1, a2 = streams[0][3], streams[1][3]
    o = a1[:, :LANES] / a1[:, LANES:] - lam * (a2[:, :LANES] / a2[:, LANES:])
    o = o * lax.rsqrt(jnp.mean(o * o, axis=-1, keepdims=True) + NORM_EPS) * g_ref[...] * sc_ref[1]
    o_ref[...] = o.astype(BF16)


def _diff_attention(p, scal, subln_g, dims, ctx_queries):
    b, s, c = dims["batch"], dims["seq"], dims["ctx"]
    tt = p.shape[0]
    qc0, kc0, vc0 = COL_DIFF_Q // LANES, COL_DIFF_K // LANES, COL_DIFF_V // LANES
    ctx_blk = (b * s) // c
    kctx = pl.BlockSpec((c, LANES), lambda bi, h, qi: (ctx_blk + bi, kc0 + h))
    vctx = pl.BlockSpec((c, LANES), lambda bi, h, qi: (ctx_blk + bi, vc0 + h))
    if ctx_queries:
        tq, nq = c, 1
        q_spec = pl.BlockSpec((tq, LANES), lambda bi, h, qi: (ctx_blk + bi, qc0 + h))
        o_spec = pl.BlockSpec((tq, LANES), lambda bi, h, qi: (bi, h))
        out_rows = b * c
        kv_specs, n_src, kv_args = [kctx, vctx], 1, (p, p)
    else:
        tq = min(DIFF_TQ, s)
        nq = s // tq
        q_spec = pl.BlockSpec((tq, LANES), lambda bi, h, qi: (bi * nq + qi, qc0 + h))
        o_spec = pl.BlockSpec((tq, LANES), lambda bi, h, qi: (bi * nq + qi, h))
        out_rows = b * s
        klat = pl.BlockSpec((s, LANES), lambda bi, h, qi: (bi, kc0 + h))
        vlat = pl.BlockSpec((s, LANES), lambda bi, h, qi: (bi, vc0 + h))
        kv_specs, n_src, kv_args = [klat, vlat, kctx, vctx], 2, (p, p, p, p)
    del tt
    n_keys = c if ctx_queries else s + c
    tk = min(DIFF_TK, n_keys)
    n_pad = -(-n_keys // tk) * tk
    return pl.pallas_call(
        functools.partial(_diff_kernel, n_src=n_src, tk=tk),
        grid=(b, DIFF_HEADS, nq),
        in_specs=[pl.BlockSpec(memory_space=pltpu.SMEM), q_spec] + kv_specs
                 + [pl.BlockSpec((1, LANES), lambda bi, h, qi: (0, 0))],
        out_specs=o_spec,
        out_shape=jax.ShapeDtypeStruct((out_rows, DIFF_HEADS * LANES), BF16),
        scratch_shapes=[pltpu.VMEM((n_pad, LANES), BF16), pltpu.VMEM((n_pad, 2 * LANES), BF16)]
                       + [pltpu.VMEM((tq, tk), F32), pltpu.VMEM((tq, tk), F32), pltpu.VMEM((tq, tk), BF16),
                          pltpu.VMEM((tq, 2 * LANES), F32), pltpu.VMEM((tq, 1), F32)] * 2,
        compiler_params=_cparams(("arbitrary", "arbitrary", "arbitrary")),
        name="diff_ctx" if ctx_queries else "diff_lat",
    )(scal, p, *kv_args, subln_g)


def _roll_bf16(a, shift):
    return pltpu.roll(a.astype(F32), shift, 1).astype(BF16)


def _gqa_core(sink_ref, q_ref, kk, vv, mask, o_ref):
    tq = q_ref.shape[0]
    lane = lax.broadcasted_iota(jnp.int32, (1, LANES), 1)
    half_of_lane = lax.shift_right_logical(lane, 6)
    kk_sw, vv_sw = _roll_bf16(kk, 64), _roll_bf16(vv, 64)
    row = lax.broadcasted_iota(jnp.int32, (2 * tq, 1), 0)
    for kvh in range(GQA_KV_HEADS):
        pairs = (2 * kvh, 2 * kvh + 1)
        outs = []
        for hh in range(2):
            kmat = kk if kvh == hh else kk_sw
            vmat = vv if kvh == hh else vv_sw
            qp = [q_ref[:, p * LANES:(p + 1) * LANES] for p in pairs]
            qs = jnp.concatenate([jnp.where(half_of_lane == hh, x, jnp.zeros_like(x)) for x in qp], axis=0)
            s = _dot_nt(qs, kmat)
            if mask is not None:
                s = jnp.where(mask, s, NEG_BIG)
            sink = jnp.where(row < tq, sink_ref[2 * pairs[0] + hh], sink_ref[2 * pairs[1] + hh])
            m = jnp.maximum(jnp.max(s, axis=-1, keepdims=True), sink)
            pr = jnp.exp(s - m)
            den = jnp.sum(pr, axis=-1, keepdims=True) + jnp.exp(sink - m)
            outs.append(_dot(pr.astype(BF16), vmat) / den)
        for pi, p in enumerate(pairs):
            rs = slice(pi * tq, (pi + 1) * tq)
            o_ref[:, p * LANES:(p + 1) * LANES] = jnp.where(half_of_lane == 0, outs[0][rs], outs[1][rs]).astype(BF16)


def _gqa_lat_kernel(sink_ref, q_ref, kp_ref, ko_ref, kn_ref, kc_ref, vp_ref, vo_ref, vn_ref, vc_ref, o_ref, *, seq):
    tq = q_ref.shape[0]
    qi = pl.program_id(1)
    kk = jnp.concatenate([kp_ref[...], ko_ref[...], kn_ref[...], kc_ref[...]], axis=0)
    vv = jnp.concatenate([vp_ref[...], vo_ref[...], vn_ref[...], vc_ref[...]], axis=0)
    nk = kk.shape[0]
    n_loc = tq + 2 * WINDOW
    r2 = lax.broadcasted_iota(jnp.int32, (2 * tq, nk), 0)
    r = jnp.where(r2 >= tq, r2 - tq, r2)
    c = lax.broadcasted_iota(jnp.int32, (2 * tq, nk), 1)
    kpos = qi * tq - WINDOW + c
    mask = ((jnp.abs(r - c + WINDOW) <= WINDOW) & (kpos >= 0) & (kpos < seq)) | (c >= n_loc)
    _gqa_core(sink_ref, q_ref, kk, vv, mask, o_ref)


def _gqa_ctx_kernel(sink_ref, q_ref, kc_ref, vc_ref, o_ref):
    _gqa_core(sink_ref, q_ref, kc_ref[...], vc_ref[...], None, o_ref)


def _gqa(p, sink, dims, ctx_queries):
    b, s, c = dims["batch"], dims["seq"], dims["ctx"]
    qw = GQA_HEADS * GQA_DH
    qc0, kc0, vc0 = COL_GQA_Q // qw, COL_GQA_K // LANES, COL_GQA_V // LANES
    ctx_blk = (b * s) // c
    sm = pl.BlockSpec(memory_space=pltpu.SMEM)
    if ctx_queries:
        return pl.pallas_call(
            _gqa_ctx_kernel,
            grid=(b,),
            in_specs=[sm, pl.BlockSpec((c, qw), lambda bi: (ctx_blk + bi, qc0)),
                      pl.BlockSpec((c, LANES), lambda bi: (ctx_blk + bi, kc0)),
                      pl.BlockSpec((c, LANES), lambda bi: (ctx_blk + bi, vc0))],
            out_specs=pl.BlockSpec((c, qw), lambda bi: (bi, 0)),
            out_shape=jax.ShapeDtypeStruct((b * c, qw), BF16),
            compiler_params=_cparams(("arbitrary",)),
            name="gqa_ctx",
        )(sink, p, p, p)
    tq = GQA_TQ
    nq = s // tq
    wpq = tq // WINDOW
    nwb = s // WINDOW

    def loc_specs(col):
        return [pl.BlockSpec((WINDOW, LANES), lambda bi, qi: (bi * nwb + jnp.maximum(qi * wpq - 1, 0), col)),
                pl.BlockSpec((tq, LANES), lambda bi, qi: (bi * nq + qi, col)),
                pl.BlockSpec((WINDOW, LANES), lambda bi, qi: (bi * nwb + jnp.minimum(qi * wpq + wpq, nwb - 1), col)),
                pl.BlockSpec((c, LANES), lambda bi, qi: (ctx_blk + bi, col))]

    return pl.pallas_call(
        functools.partial(_gqa_lat_kernel, seq=s),
        grid=(b, nq),
        in_specs=[sm, pl.BlockSpec((tq, qw), lambda bi, qi: (bi * nq + qi, qc0))] + loc_specs(kc0) + loc_specs(vc0),
        out_specs=pl.BlockSpec((tq, qw), lambda bi, qi: (bi * nq + qi, 0)),
        out_shape=jax.ShapeDtypeStruct((b * s, qw), BF16),
        compiler_params=_cparams(("arbitrary", "arbitrary")),
        name="gqa_lat",
    )(sink, p, *([p] * 8))


def _merge_kernel(x_ref, yr_ref, ydl_ref, ydc_ref, ygl_ref, ygc_ref, gmix_ref, gffn_ref, sh1_ref, sc1_ref, ga1_ref,
                  sh2_ref, sc2_ref, wg_ref, wb_ref, wo_ref, wrh_ref, wrl_ref, br_ref,
                  x1_ref, h2_ref, ti_ref, tw_ref, *, n_lat_tiles):
    d = x_ref.shape[1]
    x = x_ref[...]
    h = (_rms(x, gmix_ref[...]) * (1.0 + sc1_ref[0]) + sh1_ref[0]).astype(BF16)
    rw = RET_HEADS * RET_DV
    dw = DIFF_HEADS * LANES
    is_lat = pl.program_id(0) < n_lat_tiles
    yd = jnp.where(is_lat, ydl_ref[...], ydc_ref[...])
    yg = jnp.where(is_lat, ygl_ref[...], ygc_ref[...])
    merged = None
    for bi, (y, r0, r1) in enumerate(((yr_ref[...], 0, rw), (yd, rw, rw + dw), (yg, rw + dw, wb_ref.shape[0]))):
        gate = _sigmoid(_dot(h, wg_ref[:, bi * d:(bi + 1) * d]))
        term = gate * _dot(y, wb_ref[r0:r1, :])
        merged = term if merged is None else merged + term
    x1 = x + ga1_ref[0] * _dot(merged.astype(BF16), wo_ref[...])
    x1_ref[...] = x1
    h2 = _rms(x1, gffn_ref[...]) * (1.0 + sc2_ref[0]) + sh2_ref[0]
    tm, ns = x_ref.shape[0], d // LANES
    for s in range(ns):
        h2_ref[pl.ds(s, tm, stride=ns), :] = h2[:, s * LANES:(s + 1) * LANES]
    hh, hl = _split_bf16(h2)
    wrh, wrl = wrh_ref[...], wrl_ref[...]
    logits = _dot(hh, wrh) + (_dot(hl, wrh) + _dot(hh, wrl)) + br_ref[...]
    eidx = lax.broadcasted_iota(jnp.int32, logits.shape, 1).astype(F32)
    vals = []
    for k in range(TOP_K):
        mx = jnp.max(logits, axis=1, keepdims=True)
        ix = jnp.min(jnp.where(logits == mx, eidx, float(N_EXPERTS)), axis=1, keepdims=True)
        vals.append(mx)
        ti_ref[:, k:k + 1] = ix.astype(jnp.int32)
        logits = jnp.where(eidx == ix, -jnp.inf, logits)
    es = [jnp.exp(v - vals[0]) for v in vals]
    tot = es[0] + es[1] + es[2] + es[3]
    for k in range(TOP_K):
        tw_ref[:, k:k + 1] = es[k] / tot


def _merge(xall, yr, yd, yg, gmix, gffn, mod3, wg, wb, wo, wrh, wrl, br, dims, ntok):
    d = xall.shape[1]
    tm = dims["tm_merge"]
    tps = dims["seq"] // tm
    n_lat_tiles = dims["batch"] * tps

    def mrow(i):
        return jnp.minimum(i // tps, dims["batch"])

    def modspec(k):
        return pl.BlockSpec((1, 1, d), lambda i: (mrow(i), 0, k))

    def full(a):
        return pl.BlockSpec(a.shape, lambda i: (0,) * a.ndim)

    def rows(w):
        return pl.BlockSpec((tm, w), lambda i: (i, 0))

    def lat_ctx(pair):
        lat, ctx = pair
        w = lat.shape[1]
        lat_spec = pl.BlockSpec((tm, w), lambda i: (jnp.minimum(i, n_lat_tiles - 1), 0))
        if ctx is None:
            return [lat_spec, lat_spec], [lat, lat]
        n_ctx_tiles = ctx.shape[0] // tm
        ctx_spec = pl.BlockSpec((tm, w), lambda i: (jnp.clip(i - n_lat_tiles, 0, n_ctx_tiles - 1), 0))
        return [lat_spec, ctx_spec], [lat, ctx]

    yd_specs, yd_args = lat_ctx(yd)
    yg_specs, yg_args = lat_ctx(yg)
    return pl.pallas_call(
        functools.partial(_merge_kernel, n_lat_tiles=n_lat_tiles),
        grid=(ntok // tm,),
        in_specs=[rows(d), rows(yr.shape[1])] + yd_specs + yg_specs + [full(gmix), full(gffn),
                  modspec(0), modspec(1), modspec(2), modspec(3), modspec(4),
                  full(wg), full(wb), full(wo), full(wrh), full(wrl), full(br)],
        out_specs=[rows(d), pl.BlockSpec((tm * (d // LANES), LANES), lambda i: (i, 0)),
                   pl.BlockSpec((tm, TOP_K), lambda i: (i, 0)), pl.BlockSpec((tm, TOP_K), lambda i: (i, 0))],
        out_shape=[jax.ShapeDtypeStruct((ntok, d), F32), jax.ShapeDtypeStruct((ntok * (d // LANES), LANES), F32),
                   jax.ShapeDtypeStruct((ntok, TOP_K), jnp.int32), jax.ShapeDtypeStruct((ntok, TOP_K), F32)],
        compiler_params=_cparams(("arbitrary",)),
        name="merge",
    )(xall, yr, *yd_args, *yg_args, gmix, gffn, mod3, mod3, mod3, mod3, mod3, wg, wb, wo, wrh, wrl, br)


TOK_BITS = 15
FFN_CHUNK = 256


def _moe_kernel(be_ref, nv_ref, nu_ref, base_ref, code_ref, h2_hbm, wgu_ref, bgu_ref, wdn_ref, bdn_ref, o_hbm,
                xbuf, ybuf, wgu_bf, wdn_bf, gsem, ssem):
    i = pl.program_id(0)
    nu = nu_ref[0]
    slot = i % 2
    nb = MOE_BLOCK
    d = wdn_bf.shape[0]
    ns = d // LANES

    def gather_rows(blk, sl, r0, r1):
        base = base_ref[blk]
        for r in range(r0, r1):
            tok = code_ref[base + r] & ((1 << TOK_BITS) - 1)
            pltpu.make_async_copy(h2_hbm.at[pl.ds(pl.multiple_of(tok * ns, ns), ns)], xbuf.at[sl, pl.ds(r * ns, ns)],
                                  gsem.at[sl]).start()

    def for_rows(n, body):
        @pl.when(n == nb)
        def _():
            for r in range(nb):
                body(r)

        @pl.when(n != nb)
        def _():
            def step(r, carry):
                body(r)
                return carry
            lax.fori_loop(0, n, step, 0)

    def wait_rows(n, copy_of_rows):
        @pl.when(n == nb)
        def _():
            copy_of_rows(nb).wait()

        @pl.when(n != nb)
        def _():
            def body(r, carry):
                copy_of_rows(1).wait()
                return carry
            lax.fori_loop(0, n, body, 0)

    def gather_wait(sl):
        pltpu.make_async_copy(h2_hbm.at[pl.ds(0, nb * ns)], xbuf.at[sl], gsem.at[sl]).wait()

    def scatter_start(blk, sl):
        base = base_ref[blk]

        def body(r):
            dst = lax.shift_right_logical(code_ref[base + r], TOK_BITS)
            pltpu.make_async_copy(ybuf.at[sl, pl.ds(pl.multiple_of(r * ns, ns), ns)],
                                  o_hbm.at[pl.ds(pl.multiple_of(dst * ns, ns), ns)], ssem.at[sl]).start()
        for_rows(nv_ref[blk], body)

    def scatter_wait(blk, sl):
        wait_rows(nv_ref[blk], lambda n: pltpu.make_async_copy(
            ybuf.at[sl, pl.ds(0, n * ns)], o_hbm.at[pl.ds(0, n * ns)], ssem.at[sl]))

    @pl.when(i < nu)
    def _():
        @pl.when(i == 0)
        def _():
            gather_rows(0, 0, 0, nb)

        @pl.when((i == 0) | (be_ref[i] != be_ref[jnp.maximum(i - 1, 0)]))
        def _():
            rc = 128

            def cast(r, carry):
                r0 = pl.multiple_of(r * rc, rc)
                wgu_bf[pl.ds(r0, rc), :] = wgu_ref[0, 0, pl.ds(r0, rc), :].astype(BF16)
                wdn_bf[pl.ds(r0, rc), :] = wdn_ref[0, 0, pl.ds(r0, rc), :].astype(BF16)
                return carry
            lax.fori_loop(0, d // rc, cast, 0)

        gather_wait(slot)
        nxt = jnp.minimum(i + 1, pl.num_programs(0) - 1)
        x = jnp.concatenate([xbuf[slot, pl.ds(s, nb, stride=ns), :] for s in range(ns)], axis=1).astype(BF16)
        cw = FFN_CHUNK
        nch = d // cw
        y = None
        for ci in range(nch):
            c0, c1 = ci * cw, (ci + 1) * cw
            glu = jnp.minimum(_dot(x, wgu_bf[:, c0:c1]) + bgu_ref[0, 0, :, c0:c1], SWIGLU_LIMIT)
            lin = jnp.clip(_dot(x, wgu_bf[:, d + c0:d + c1]) + bgu_ref[0, 0, :, d + c0:d + c1],
                           -SWIGLU_LIMIT, SWIGLU_LIMIT)
            act = glu * _sigmoid(SWIGLU_ALPHA * glu) * (lin + 1.0)
            part = _dot(act.astype(BF16), wdn_bf[c0:c1, :])
            y = part if y is None else y + part
            gather_rows(nxt, 1 - slot, ci * (nb // nch), (ci + 1) * (nb // nch))
        y = y + bdn_ref[0, 0]

        @pl.when(i >= 2)
        def _():
            scatter_wait(i - 2, slot)

        for s in range(ns):
            ybuf[slot, pl.ds(s, nb, stride=ns), :] = y[:, s * LANES:(s + 1) * LANES]
        scatter_start(i, slot)

        @pl.when(i == nu - 1)
        def _():
            gather_wait(1 - slot)

            @pl.when(i >= 1)
            def _():
                scatter_wait(i - 1, 1 - slot)
            scatter_wait(i, slot)


def _moe(h2, route, w_gu, b_gu, w_dn, b_dn, layer):
    codes, block_expert, n_valid, n_used, base = route
    d = w_dn.shape[-1]
    ns = d // LANES
    ntok = h2.shape[0] // ns
    n_blocks = block_expert.shape[0]
    depth, ne = w_gu.shape[:2]

    def wspec(rows, cols):
        return pl.BlockSpec((1, 1, rows, cols), lambda i, be, nv, nu, bs, cd: (layer, be[i], 0, 0))

    grid_spec = pltpu.PrefetchScalarGridSpec(
        num_scalar_prefetch=5,
        grid=(n_blocks,),
        in_specs=[pl.BlockSpec(memory_space=pl.ANY),
                  wspec(d, 2 * d), wspec(1, 2 * d), wspec(d, d), wspec(1, d)],
        out_specs=pl.BlockSpec(memory_space=pl.ANY),
        scratch_shapes=[pltpu.VMEM((2, MOE_BLOCK * ns, LANES), F32), pltpu.VMEM((2, MOE_BLOCK * ns, LANES), F32),
                        pltpu.VMEM((d, 2 * d), BF16), pltpu.VMEM((d, d), BF16),
                        pltpu.SemaphoreType.DMA((2,)), pltpu.SemaphoreType.DMA((2,))])
    return pl.pallas_call(
        _moe_kernel,
        grid_spec=grid_spec,
        out_shape=jax.ShapeDtypeStruct((TOP_K * ntok * ns, LANES), F32),
        compiler_params=_cparams(("arbitrary",)),
        name="moe",
    )(block_expert, n_valid, n_used, base, codes, h2, w_gu, b_gu.reshape(depth, ne, 1, 2 * d), w_dn,
      b_dn.reshape(depth, ne, 1, d))


def _route(ti, ntok):
    a = TOP_K * ntok
    n_blocks = (a + N_EXPERTS * (MOE_BLOCK - 1) + MOE_BLOCK - 1) // MOE_BLOCK
    expert = ti.T.reshape(a)
    aid = jnp.arange(a, dtype=jnp.int32)
    _, codes = lax.sort((expert, lax.shift_left(aid, TOK_BITS) | (aid % ntok)), num_keys=1)
    codes = jnp.concatenate([codes, jnp.zeros((MOE_BLOCK,), jnp.int32)])
    counts = jnp.sum((expert[:, None] == jnp.arange(N_EXPERTS, dtype=jnp.int32)[None, :]).astype(jnp.int32), axis=0)
    padded = (counts + MOE_BLOCK - 1) // MOE_BLOCK * MOE_BLOCK
    pad_end = jnp.cumsum(padded)
    pad_start = pad_end - padded
    cstart = jnp.cumsum(counts) - counts
    n_used = (pad_end[-1:] // MOE_BLOCK).astype(jnp.int32)
    block_start = jnp.arange(n_blocks, dtype=jnp.int32) * MOE_BLOCK
    block_expert = jnp.minimum(jnp.sum((pad_end[None, :] <= block_start[:, None]).astype(jnp.int32), axis=1),
                               N_EXPERTS - 1)
    n_valid = jnp.clip(counts[block_expert] - (block_start - pad_start[block_expert]), 0, MOE_BLOCK).astype(jnp.int32)
    base = jnp.clip(block_start - (pad_start - cstart)[block_expert], 0, a).astype(jnp.int32)
    return codes, block_expert, n_valid, n_used, base


def _combine_kernel(x1_ref, o0_ref, o1_ref, o2_ref, o3_ref, tw_ref, ga2_ref, gf_ref, x2_ref, *, final):
    o_refs = (o0_ref, o1_ref, o2_ref, o3_ref)
    tm, ns = x1_ref.shape[0], x1_ref.shape[1] // LANES
    w = [jnp.broadcast_to(tw_ref[:, k:k + 1], (tm, LANES)) for k in range(TOP_K)]
    for s in range(ns):
        cols = slice(s * LANES, (s + 1) * LANES)
        y = None
        for k in range(TOP_K):
            term = w[k] * o_refs[k][pl.ds(s, tm, stride=ns), :]
            y = term if y is None else y + term
        x2_ref[:, cols] = x1_ref[:, cols] + ga2_ref[0, :, cols] * y
    if final:
        x2_ref[...] = _rms(x2_ref[...], gf_ref[...])


def _combine(x1, o4, tw, mod3, g_final, dims, final):
    ntok, d = x1.shape
    tm = dims["tm_merge"]
    tps = dims["seq"] // tm
    nt = ntok // tm

    def ospec(k):
        return pl.BlockSpec((tm * (d // LANES), LANES), lambda i: (k * nt + i, 0))

    return pl.pallas_call(
        functools.partial(_combine_kernel, final=final),
        grid=(nt,),
        in_specs=[pl.BlockSpec((tm, d), lambda i: (i, 0)), ospec(0), ospec(1), ospec(2), ospec(3),
                  pl.BlockSpec((tm, TOP_K), lambda i: (i, 0)),
                  pl.BlockSpec((1, 1, d), lambda i: (jnp.minimum(i // tps, dims["batch"]), 0, 5)),
                  pl.BlockSpec((1, d), lambda i: (0, 0))],
        out_specs=pl.BlockSpec((tm, d), lambda i: (i, 0)),
        out_shape=jax.ShapeDtypeStruct((ntok, d), F32),
        compiler_params=_cparams(("arbitrary",)),
        name="combine",
    )(x1, o4, o4, o4, o4, tw, mod3, g_final)


def _rope_tables(seq, pad_rows):
    pos = jnp.arange(seq, dtype=jnp.int32)
    row = (pos // GRID_W).astype(F32)
    col = (pos % GRID_W).astype(F32)

    def cs(head_dim):
        quarter = head_dim // 4
        inv_freq = ROPE_BASE ** (-jnp.arange(quarter, dtype=F32) / quarter)
        ang = jnp.concatenate([row[:, None] * inv_freq, col[:, None] * inv_freq], axis=-1)
        return jnp.cos(ang), jnp.sin(ang)

    c128, s128 = cs(RET_DK)
    c64, s64 = cs(DIFF_DK)
    ones = jnp.ones((pad_rows, LANES), F32)
    zeros = jnp.zeros((pad_rows, LANES), F32)
    a128 = jnp.concatenate([jnp.concatenate([c128, c128], -1), ones], 0)
    b128 = jnp.concatenate([jnp.concatenate([-s128, s128], -1), zeros], 0)
    a64 = jnp.concatenate([jnp.concatenate([c64, c64, c64, c64], -1), ones], 0)
    b64 = jnp.concatenate([jnp.concatenate([-s64, s64, -s64, s64], -1), zeros], 0)
    return a128, b128, a64, b64


def _pick_tile(cands, *sizes):
    for t in cands:
        if all(s % t == 0 for s in sizes):
            return t
    raise ValueError("no tile size fits")


def kernel(x, c, ctx, c_ctx, w_mod, b_mod, g_norm_mix, g_norm_ffn, w_in, ret_decay_fwd, ret_decay_bwd, diff_lambda_q1, diff_lambda_k1, diff_lambda_q2, diff_lambda_k2, diff_subln_g, gqa_sink, w_branch, w_out, w_router, b_router, w_gate_up, b_gate_up, w_down, b_down, g_final):
    b, s, d = x.shape
    n_ctx = ctx.shape[1]
    depth = w_mod.shape[0]
    n_lat = b * s
    tt = n_lat + b * n_ctx
    assert s % RET_CHUNK == 0 and n_ctx % RET_CHUNK == 0 and s % GQA_TQ == 0 and n_lat % n_ctx == 0
    assert tt < (1 << TOK_BITS) and b < 8
    tm = _pick_tile((1024, 512, 256), s, b * n_ctx)
    tm_merge = _pick_tile((512, 256), s, b * n_ctx)
    dims = dict(batch=b, seq=s, ctx=n_ctx, tm=tm, tm_merge=tm_merge,
                n_lat_tiles=n_lat // tm, tiles_per_seq=s // tm)

    xall = jnp.concatenate([x.reshape(n_lat, d), ctx.reshape(b * n_ctx, d)], axis=0)
    cvec = jnp.concatenate([c, c_ctx[None, :], jnp.zeros((8 - b - 1, d), F32)], axis=0)
    mod = _mod_all(cvec, w_mod, b_mod)
    tabs = _rope_tables(s, tm)

    for l in range(depth):
        last = l == depth - 1
        mod3 = mod[l].reshape(8, 1, 6 * d)
        w_l = w_in[l]
        p = _inproj(xall, g_norm_mix[l][None, :], mod3, w_l[:, :P_COLS].astype(BF16), tabs, dims)

        lg = jnp.stack([jax.nn.log_sigmoid(ret_decay_fwd[l].astype(F32)),
                        jax.nn.log_sigmoid(ret_decay_bwd[l].astype(F32))], axis=0)
        y_r = _retention(p, lg, dims)

        lam_init = 0.8 - 0.6 * math.exp(-0.3 * l)
        lam = (jnp.exp(jnp.sum(diff_lambda_q1[l].astype(F32) * diff_lambda_k1[l].astype(F32)))
               - jnp.exp(jnp.sum(diff_lambda_q2[l].astype(F32) * diff_lambda_k2[l].astype(F32))) + lam_init)
        scal = jnp.stack([lam, jnp.asarray(1.0 - lam_init, F32)])
        subln = diff_subln_g[l][None, :].astype(F32)
        sink = gqa_sink[l].astype(F32)
        y_d = (_diff_attention(p, scal, subln, dims, False),
               None if last else _diff_attention(p, scal, subln, dims, True))
        y_g = (_gqa(p, sink, dims, False), None if last else _gqa(p, sink, dims, True))
        ntok = n_lat if last else tt

        wrh, wrl = _split_bf16(w_router[l])
        x1, h2, ti, tw = _merge(xall, y_r, y_d, y_g, g_norm_mix[l][None, :], g_norm_ffn[l][None, :], mod3,
                                w_l[:, P_COLS:].astype(BF16), w_branch[l].astype(BF16), w_out[l].astype(BF16),
                                wrh, wrl, b_router[l][None, :].astype(F32), dims, ntok)
        o4 = _moe(h2, _route(ti, ntok), w_gate_up, b_gate_up, w_down, b_down, l)
        xall = _combine(x1, o4, tw, mod3, g_final[None, :], dims, last)
    return xall.reshape(b, s, d)
```

```python
import functools
import math

import jax
import jax.numpy as jnp
from jax import lax
from jax.experimental import pallas as pl
from jax.experimental.pallas import tpu as pltpu

F32 = jnp.float32
BF16 = jnp.bfloat16

GRID_W = 64
RET_HEADS, RET_DK, RET_DV = 4, 128, 256
DIFF_HEADS, DIFF_DK = 4, 64
GQA_HEADS, GQA_KV_HEADS, GQA_DH = 8, 2, 64
WINDOW = 128
N_EXPERTS, TOP_K = 32, 4
SWIGLU_ALPHA, SWIGLU_LIMIT = 1.702, 7.0
MOE_BLOCK = 256
ROPE_BASE = 10000.0
NORM_EPS = 1e-6
NEG_BIG = -1e30

LANES = 128
P_COLS = 5376
IN_TN = 768
RET_CHUNK = 256
DIFF_TQ, DIFF_TK = 512, 512
DIFF_ROW_CHUNK = 64
LOG2E = math.log2(math.e)
GQA_TQ = 256
VMEM_LIMIT = 56 * 1024 * 1024

_PLAIN = ("plain", 1.0)
_GROUP_KINDS = ([("r128", RET_DK ** -0.5)] * 4 + [("r128", 1.0)] * 4 + [_PLAIN] * 16
                + [("r64", DIFF_DK ** -0.5 * LOG2E)] * 4 + [("r64", 1.0)] * 4 + [_PLAIN] * 4
                + [("r64", GQA_DH ** -0.5)] * 4 + [("r64", 1.0)] + [_PLAIN])
COL_RET_Q, COL_RET_K, COL_RET_V, COL_RET_G = 0, 512, 1024, 2048
COL_DIFF_Q, COL_DIFF_K, COL_DIFF_V = 3072, 3584, 4096
COL_GQA_Q, COL_GQA_K, COL_GQA_V = 4608, 5120, 5248


def _cparams(sem):
    return pltpu.CompilerParams(dimension_semantics=sem, vmem_limit_bytes=VMEM_LIMIT)


def _sigmoid(x):
    return 1.0 / (1.0 + jnp.exp(-x))


def _dot(a, b):
    return jnp.dot(a, b, preferred_element_type=F32)


def _dot_nt(a, b):
    return lax.dot_general(a, b, (((1,), (1,)), ((), ())), preferred_element_type=F32)


def _dot_tn(a, b):
    return lax.dot_general(a, b, (((0,), (0,)), ((), ())), preferred_element_type=F32)


def _split_bf16(a):
    hi = a.astype(BF16)
    lo = (a - hi.astype(F32)).astype(BF16)
    return hi, lo


def _rms(x, g):
    return x * lax.rsqrt(jnp.mean(x * x, axis=-1, keepdims=True) + NORM_EPS) * g


def _mod_kernel(c_ref, w_ref, b_ref, o_ref):
    c = c_ref[...]
    s = c * _sigmoid(c)
    sh, sl = _split_bf16(s)
    wh, wl = _split_bf16(w_ref[0])
    o_ref[0] = _dot(sh, wh) + (_dot(sl, wh) + _dot(sh, wl)) + b_ref[0]


def _mod_all(cvec, w_mod, b_mod):
    depth, d, n = w_mod.shape
    tn = n // 4
    return pl.pallas_call(
        _mod_kernel,
        grid=(depth, n // tn),
        in_specs=[pl.BlockSpec((8, d), lambda l, j: (0, 0)),
                  pl.BlockSpec((1, d, tn), lambda l, j: (l, 0, j)),
                  pl.BlockSpec((1, 1, tn), lambda l, j: (l, 0, j))],
        out_specs=pl.BlockSpec((1, 8, tn), lambda l, j: (l, 0, j)),
        out_shape=jax.ShapeDtypeStruct((depth, 8, n), F32),
        compiler_params=_cparams(("arbitrary", "arbitrary")),
        name="mod",
    )(cvec, w_mod, b_mod.reshape(depth, 1, n))


def _rope_group(a, kind, scale, a128, b128, a64, b64):
    if kind == "plain":
        return a
    if kind == "r128":
        out = a * a128 + pltpu.roll(a, 64, 1) * b128
    else:
        lane = lax.broadcasted_iota(jnp.int32, (1, LANES), 1)
        first = (lane & 63) < 32
        partner = jnp.where(first, pltpu.roll(a, 96, 1), pltpu.roll(a, 32, 1))
        out = a * a64 + partner * b64
    return out * scale if scale != 1.0 else out


def _inproj_kernel(x_ref, g_ref, sc_ref, sh_ref, w_ref, a128_ref, b128_ref, a64_ref, b64_ref,
                   o_ref, h_scr, acc_scr, *, n_tiles):
    j = pl.program_id(1)

    @pl.when(j == 0)
    def _():
        y = _rms(x_ref[...], g_ref[...])
        h_scr[...] = (y * (1.0 + sc_ref[0]) + sh_ref[0]).astype(BF16)

    acc_scr[...] = _dot(h_scr[...], w_ref[0].astype(BF16))
    gpt = IN_TN // LANES
    for jj in range(n_tiles):
        @pl.when(j == jj)
        def _(jj=jj):
            for g in range(gpt):
                kind, scale = _GROUP_KINDS[jj * gpt + g]
                a = acc_scr[:, g * LANES:(g + 1) * LANES]
                out = _rope_group(a, kind, scale, a128_ref[...], b128_ref[...], a64_ref[...], b64_ref[...])
                o_ref[:, g * LANES:(g + 1) * LANES] = out.astype(BF16)


def _inproj(xall, g, mod3, w_in, layer, tabs, dims):
    tt, d = xall.shape
    tm = dims["tm"]
    n_lat_tiles, tps = dims["n_lat_tiles"], dims["tiles_per_seq"]
    n_tiles = P_COLS // IN_TN

    def mrow(i):
        return jnp.minimum(i // tps, dims["batch"])

    def trow(i):
        return jnp.where(i < n_lat_tiles, i % tps, tps)

    tab_spec = pl.BlockSpec((tm, LANES), lambda i, j: (trow(i), 0))
    return pl.pallas_call(
        functools.partial(_inproj_kernel, n_tiles=n_tiles),
        grid=(tt // tm, n_tiles),
        in_specs=[pl.BlockSpec((tm, d), lambda i, j: (i, 0)),
                  pl.BlockSpec((1, d), lambda i, j: (0, 0)),
                  pl.BlockSpec((1, 1, d), lambda i, j: (mrow(i), 0, 1)),
                  pl.BlockSpec((1, 1, d), lambda i, j: (mrow(i), 0, 0)),
                  pl.BlockSpec((1, d, IN_TN), lambda i, j: (layer, 0, j)),
                  tab_spec, tab_spec, tab_spec, tab_spec],
        out_specs=pl.BlockSpec((tm, IN_TN), lambda i, j: (i, j)),
        out_shape=jax.ShapeDtypeStruct((tt, P_COLS), BF16),
        scratch_shapes=[pltpu.VMEM((tm, d), BF16), pltpu.VMEM((tm, IN_TN), F32)],
        compiler_params=_cparams(("arbitrary", "arbitrary")),
        name="inproj",
    )(xall, g, mod3, mod3, w_in, *tabs)


def _ret_kernel(lg_ref, q_ref, k_ref, v_ref, g_ref, y_ref, s_scr, of_scr, *, ncc, ncl):
    ph = pl.program_id(1)
    t = pl.program_id(2)
    ch = RET_CHUNK

    @pl.when(t == 0)
    def _():
        s_scr[...] = jnp.zeros_like(s_scr)

    fwd = ph == 0
    ii = lax.broadcasted_iota(jnp.int32, (ch, ch), 0).astype(F32)
    jj = lax.broadcasted_iota(jnp.int32, (ch, ch), 1).astype(F32)
    rel = jnp.where(fwd, ii - jj, jj - ii)
    pos = lax.broadcasted_iota(jnp.int32, (ch, 1), 0).astype(F32)
    qpos = jnp.where(fwd, pos + 1.0, ch - pos)
    kpos = jnp.where(fwd, ch - 1.0 - pos, pos)
    mirror = jnp.where(t < ncc, ncc - 1 - t, ncc + ncl - 1 - (t - ncc))
    row0 = pl.multiple_of(jnp.where(fwd, t, mirror) * ch, ch)

    for h in range(RET_HEADS):
        lg = lg_ref[ph, h]
        dmask = jnp.where(rel >= 0.0, jnp.exp(jnp.maximum(rel, 0.0) * lg), 0.0)
        q = q_ref[:, h * RET_DK:(h + 1) * RET_DK]
        k = k_ref[:, h * RET_DK:(h + 1) * RET_DK]
        v = v_ref[:, h * RET_DV:(h + 1) * RET_DV]
        a = (_dot_nt(q, k) * dmask).astype(BF16)
        qd = (q.astype(F32) * jnp.exp(qpos * lg)).astype(BF16)
        kd = (k.astype(F32) * jnp.exp(kpos * lg)).astype(BF16)
        state = s_scr[h]
        o = _dot(a, v) + _dot(qd, state.astype(BF16))
        s_scr[h] = state * jnp.exp(ch * lg) + _dot_tn(kd, v)
        cs = slice(h * RET_DV, (h + 1) * RET_DV)

        @pl.when(fwd)
        def _(o=o, cs=cs):
            of_scr[pl.ds(row0, ch), cs] = o

        @pl.when(jnp.logical_not(fwd))
        def _(o=o, cs=cs):
            ot = o + of_scr[pl.ds(row0, ch), cs]
            mu = jnp.mean(ot, axis=-1, keepdims=True)
            oc = ot - mu
            var = jnp.mean(oc * oc, axis=-1, keepdims=True)
            on = oc * lax.rsqrt(var + NORM_EPS)
            gate = g_ref[:, cs].astype(F32)
            y_ref[:, cs] = (gate * _sigmoid(gate) * on).astype(BF16)


def _retention(p, lg, dims):
    tt = p.shape[0]
    ch = RET_CHUNK
    b, s, c = dims["batch"], dims["seq"], dims["ctx"]
    ncc, ncl = c // ch, s // ch
    nsteps = ncc + ncl
    ctx_base = (b * s) // ch

    def rblk(bi, ph, t):
        fwd_blk = jnp.where(t < ncc, ctx_base + bi * ncc + t, bi * ncl + (t - ncc))
        bwd_blk = jnp.where(t < ncc, ctx_base + bi * ncc + (ncc - 1 - t), bi * ncl + (ncl - 1 - (t - ncc)))
        return jnp.where(ph == 0, fwd_blk, bwd_blk)

    def oblk(bi, ph, t):
        return rblk(bi, 1, jnp.where(ph == 0, 0, t))

    qk_w = RET_HEADS * RET_DK
    v_w = RET_HEADS * RET_DV
    return pl.pallas_call(
        functools.partial(_ret_kernel, ncc=ncc, ncl=ncl),
        grid=(b, 2, nsteps),
        in_specs=[pl.BlockSpec(memory_space=pltpu.SMEM),
                  pl.BlockSpec((ch, qk_w), lambda bi, ph, t: (rblk(bi, ph, t), COL_RET_Q // qk_w)),
                  pl.BlockSpec((ch, qk_w), lambda bi, ph, t: (rblk(bi, ph, t), COL_RET_K // qk_w)),
                  pl.BlockSpec((ch, v_w), lambda bi, ph, t: (rblk(bi, ph, t), COL_RET_V // v_w)),
                  pl.BlockSpec((ch, v_w), lambda bi, ph, t: (rblk(bi, ph, t), COL_RET_G // v_w))],
        out_specs=pl.BlockSpec((ch, v_w), lambda bi, ph, t: (oblk(bi, ph, t), 0)),
        out_shape=jax.ShapeDtypeStruct((tt, v_w), BF16),
        scratch_shapes=[pltpu.VMEM((RET_HEADS, RET_DK, RET_DV), F32),
                        pltpu.VMEM((nsteps * ch, v_w), F32)],
        compiler_params=_cparams(("arbitrary", "arbitrary", "arbitrary")),
        name="retention",
    )(lg, p, p, p, p)


def _diff_kernel(sc_ref, q_ref, *refs, n_src, tk):
    kv = refs[:2 * n_src]
    g_ref, o_ref = refs[2 * n_src], refs[2 * n_src + 1]
    kx_scr, vx_scr = refs[2 * n_src + 2], refs[2 * n_src + 3]
    nbuf = 8
    streams = (refs[2 * n_src + 4:2 * n_src + 4 + nbuf], refs[2 * n_src + 4 + nbuf:2 * n_src + 4 + 2 * nbuf])
    tq = q_ref.shape[0]
    rc = DIFF_ROW_CHUNK
    n_keys = sum(kv[2 * si].shape[0] for si in range(n_src))
    n_tiles = kx_scr.shape[0] // tk

    @pl.when(pl.program_id(2) == 0)
    def _():
        vx_scr[:, LANES:] = jnp.ones((vx_scr.shape[0], LANES), BF16)
        r0 = 0
        for si in range(n_src):
            k_ref, v_ref = kv[2 * si], kv[2 * si + 1]
            kx_scr[r0:r0 + k_ref.shape[0], :] = k_ref[...]
            vx_scr[r0:r0 + v_ref.shape[0], :LANES] = v_ref[...]
            r0 += k_ref.shape[0]
        if r0 < kx_scr.shape[0]:
            kx_scr[r0:, :] = jnp.zeros((kx_scr.shape[0] - r0, LANES), BF16)
            vx_scr[r0:, :LANES] = jnp.zeros((kx_scr.shape[0] - r0, LANES), BF16)

    q = q_ref[...]
    lane = lax.broadcasted_iota(jnp.int32, (1, LANES), 1)
    zero = jnp.zeros_like(q)
    qs = (jnp.where(lane < DIFF_DK, q, zero), jnp.where(lane >= DIFF_DK, q, zero))
    S_BUF, P_BUF, AL_BUF, ACC, MAX = 0, 2, 4, 6, 7
    for bufs in streams:
        bufs[ACC][...] = jnp.zeros_like(bufs[ACC])
        bufs[MAX][...] = jnp.full(bufs[MAX].shape, NEG_BIG, F32)

    def tile_rows(t):
        return pl.ds(t * tk if isinstance(t, int) else pl.multiple_of(t * tk, tk), tk)

    def scores(t, par):
        kt = kx_scr[tile_rows(t), :]
        for st, bufs in enumerate(streams):
            bufs[S_BUF + par][...] = _dot_nt(qs[st], kt)

    def accumulate(t, par):
        vt = vx_scr[tile_rows(t), :]
        for bufs in streams:
            bufs[ACC][...] = bufs[AL_BUF + par][...] * bufs[ACC][...] + _dot(bufs[P_BUF + par][...], vt)

    def stage(t, par, prefetch, n_valid, drain_prev):
        if prefetch:
            scores(t + 1, 1 - par)
        if drain_prev:
            accumulate(t - 1, 1 - par)
        for bufs in streams:
            s_scr, p_scr, al_scr, m_scr = bufs[S_BUF + par], bufs[P_BUF + par], bufs[AL_BUF + par], bufs[MAX]
            m_all = m_scr[...]
            m_new, alpha = [], []
            for r in range(tq // rc):
                rows = slice(r * rc, (r + 1) * rc)
                s = s_scr[rows, :]
                if n_valid < tk:
                    col = lax.broadcasted_iota(jnp.int32, s.shape, 1)
                    s = jnp.where(col < n_valid, s, NEG_BIG)
                m_old = m_all[rows]
                mn = jnp.maximum(m_old, jnp.max(s, axis=-1, keepdims=True))
                p_scr[rows, :] = jnp.exp2(s - mn).astype(BF16)
                alpha.append(jnp.exp2(m_old - mn))
                m_new.append(mn)
            m_scr[...] = jnp.concatenate(m_new, axis=0)
            al_scr[...] = jnp.concatenate(alpha, axis=0)

    def n_valid_of(t):
        return n_keys - t * tk if t == n_tiles - 1 else tk

    scores(0, 0)
    stage(0, 0, n_tiles > 1, n_valid_of(0), False)
    pairs = max(0, (n_tiles - 2) // 2)
    if pairs > 0:
        def body(u, carry):
            stage(2 * u + 1, 1, True, tk, True)
            stage(2 * u + 2, 0, True, tk, True)
            return carry
        lax.fori_loop(0, pairs, body, 0)
    for t in range(2 * pairs + 1, n_tiles):
        stage(t, t % 2, t < n_tiles - 1, n_valid_of(t), True)
    accumulate(n_tiles - 1, (n_tiles - 1) % 2)
    lam = sc_ref[0]
    a1, a2 = streams[0][ACC], streams[1][ACC]
    o = a1[:, :LANES] / a1[:, LANES:] - lam * (a2[:, :LANES] / a2[:, LANES:])
    o = o * lax.rsqrt(jnp.mean(o * o, axis=-1, keepdims=True) + NORM_EPS) * g_ref[...] * sc_ref[1]
    o_ref[...] = o.astype(BF16)


def _diff_attention(p, scal, subln_g, dims, ctx_queries):
    b, s, c = dims["batch"], dims["seq"], dims["ctx"]
    tt = p.shape[0]
    qc0, kc0, vc0 = COL_DIFF_Q // LANES, COL_DIFF_K // LANES, COL_DIFF_V // LANES
    ctx_blk = (b * s) // c
    kctx = pl.BlockSpec((c, LANES), lambda bi, h, qi: (ctx_blk + bi, kc0 + h))
    vctx = pl.BlockSpec((c, LANES), lambda bi, h, qi: (ctx_blk + bi, vc0 + h))
    if ctx_queries:
        tq, nq = c, 1
        q_spec = pl.BlockSpec((tq, LANES), lambda bi, h, qi: (ctx_blk + bi, qc0 + h))
        o_spec = pl.BlockSpec((tq, LANES), lambda bi, h, qi: (bi, h))
        out_rows = b * c
        kv_specs, n_src, kv_args = [kctx, vctx], 1, (p, p)
    else:
        tq = min(DIFF_TQ, s)
        nq = s // tq
        q_spec = pl.BlockSpec((tq, LANES), lambda bi, h, qi: (bi * nq + qi, qc0 + h))
        o_spec = pl.BlockSpec((tq, LANES), lambda bi, h, qi: (bi * nq + qi, h))
        out_rows = b * s
        klat = pl.BlockSpec((s, LANES), lambda bi, h, qi: (bi, kc0 + h))
        vlat = pl.BlockSpec((s, LANES), lambda bi, h, qi: (bi, vc0 + h))
        kv_specs, n_src, kv_args = [klat, vlat, kctx, vctx], 2, (p, p, p, p)
    del tt
    n_keys = c if ctx_queries else s + c
    tk = min(DIFF_TK, n_keys)
    n_pad = -(-n_keys // tk) * tk
    return pl.pallas_call(
        functools.partial(_diff_kernel, n_src=n_src, tk=tk),
        grid=(b, DIFF_HEADS, nq),
        in_specs=[pl.BlockSpec(memory_space=pltpu.SMEM), q_spec] + kv_specs
                 + [pl.BlockSpec((1, LANES), lambda bi, h, qi: (0, 0))],
        out_specs=o_spec,
        out_shape=jax.ShapeDtypeStruct((out_rows, DIFF_HEADS * LANES), BF16),
        scratch_shapes=[pltpu.VMEM((n_pad, LANES), BF16), pltpu.VMEM((n_pad, 2 * LANES), BF16)]
                       + [pltpu.VMEM((tq, tk), F32), pltpu.VMEM((tq, tk), F32),
                          pltpu.VMEM((tq, tk), BF16), pltpu.VMEM((tq, tk), BF16),
                          pltpu.VMEM((tq, 1), F32), pltpu.VMEM((tq, 1), F32),
                          pltpu.VMEM((tq, 2 * LANES), F32), pltpu.VMEM((tq, 1), F32)] * 2,
        compiler_params=_cparams(("arbitrary", "arbitrary", "arbitrary")),
        name="diff_ctx" if ctx_queries else "diff_lat",
    )(scal, p, *kv_args, subln_g)


def _roll_bf16(a, shift):
    return pltpu.roll(a.astype(F32), shift, 1).astype(BF16)


def _gqa_core(sink_ref, q_ref, kk, vv, mask, o_ref):
    tq = q_ref.shape[0]
    lane = lax.broadcasted_iota(jnp.int32, (1, LANES), 1)
    half_of_lane = lax.shift_right_logical(lane, 6)
    kk_sw, vv_sw = _roll_bf16(kk, 64), _roll_bf16(vv, 64)
    row = lax.broadcasted_iota(jnp.int32, (2 * tq, 1), 0)
    for kvh in range(GQA_KV_HEADS):
        pairs = (2 * kvh, 2 * kvh + 1)
        outs = []
        for hh in range(2):
            kmat = kk if kvh == hh else kk_sw
            vmat = vv if kvh == hh else vv_sw
            qp = [q_ref[:, p * LANES:(p + 1) * LANES] for p in pairs]
            qs = jnp.concatenate([jnp.where(half_of_lane == hh, x, jnp.zeros_like(x)) for x in qp], axis=0)
            s = _dot_nt(qs, kmat)
            if mask is not None:
                s = jnp.where(mask, s, NEG_BIG)
            sink = jnp.where(row < tq, sink_ref[2 * pairs[0] + hh], sink_ref[2 * pairs[1] + hh])
            m = jnp.maximum(jnp.max(s, axis=-1, keepdims=True), sink)
            pr = jnp.exp(s - m)
            den = jnp.sum(pr, axis=-1, keepdims=True) + jnp.exp(sink - m)
            outs.append(_dot(pr.astype(BF16), vmat) / den)
        for pi, p in enumerate(pairs):
            rs = slice(pi * tq, (pi + 1) * tq)
            o_ref[:, p * LANES:(p + 1) * LANES] = jnp.where(half_of_lane == 0, outs[0][rs], outs[1][rs]).astype(BF16)


def _gqa_lat_kernel(sink_ref, q_ref, kp_ref, ko_ref, kn_ref, kc_ref, vp_ref, vo_ref, vn_ref, vc_ref, o_ref, *, seq):
    tq = q_ref.shape[0]
    qi = pl.program_id(1)
    kk = jnp.concatenate([kp_ref[...], ko_ref[...], kn_ref[...], kc_ref[...]], axis=0)
    vv = jnp.concatenate([vp_ref[...], vo_ref[...], vn_ref[...], vc_ref[...]], axis=0)
    nk = kk.shape[0]
    n_loc = tq + 2 * WINDOW
    r2 = lax.broadcasted_iota(jnp.int32, (2 * tq, nk), 0)
    r = jnp.where(r2 >= tq, r2 - tq, r2)
    c = lax.broadcasted_iota(jnp.int32, (2 * tq, nk), 1)
    kpos = qi * tq - WINDOW + c
    mask = ((jnp.abs(r - c + WINDOW) <= WINDOW) & (kpos >= 0) & (kpos < seq)) | (c >= n_loc)
    _gqa_core(sink_ref, q_ref, kk, vv, mask, o_ref)


def _gqa_ctx_kernel(sink_ref, q_ref, kc_ref, vc_ref, o_ref):
    _gqa_core(sink_ref, q_ref, kc_ref[...], vc_ref[...], None, o_ref)


def _gqa(p, sink, dims, ctx_queries):
    b, s, c = dims["batch"], dims["seq"], dims["ctx"]
    qw = GQA_HEADS * GQA_DH
    qc0, kc0, vc0 = COL_GQA_Q // qw, COL_GQA_K // LANES, COL_GQA_V // LANES
    ctx_blk = (b * s) // c
    sm = pl.BlockSpec(memory_space=pltpu.SMEM)
    if ctx_queries:
        return pl.pallas_call(
            _gqa_ctx_kernel,
            grid=(b,),
            in_specs=[sm, pl.BlockSpec((c, qw), lambda bi: (ctx_blk + bi, qc0)),
                      pl.BlockSpec((c, LANES), lambda bi: (ctx_blk + bi, kc0)),
                      pl.BlockSpec((c, LANES), lambda bi: (ctx_blk + bi, vc0))],
            out_specs=pl.BlockSpec((c, qw), lambda bi: (bi, 0)),
            out_shape=jax.ShapeDtypeStruct((b * c, qw), BF16),
            compiler_params=_cparams(("arbitrary",)),
            name="gqa_ctx",
        )(sink, p, p, p)
    tq = GQA_TQ
    nq = s // tq
    wpq = tq // WINDOW
    nwb = s // WINDOW

    def loc_specs(col):
        return [pl.BlockSpec((WINDOW, LANES), lambda bi, qi: (bi * nwb + jnp.maximum(qi * wpq - 1, 0), col)),
                pl.BlockSpec((tq, LANES), lambda bi, qi: (bi * nq + qi, col)),
                pl.BlockSpec((WINDOW, LANES), lambda bi, qi: (bi * nwb + jnp.minimum(qi * wpq + wpq, nwb - 1), col)),
                pl.BlockSpec((c, LANES), lambda bi, qi: (ctx_blk + bi, col))]

    return pl.pallas_call(
        functools.partial(_gqa_lat_kernel, seq=s),
        grid=(b, nq),
        in_specs=[sm, pl.BlockSpec((tq, qw), lambda bi, qi: (bi * nq + qi, qc0))] + loc_specs(kc0) + loc_specs(vc0),
        out_specs=pl.BlockSpec((tq, qw), lambda bi, qi: (bi * nq + qi, 0)),
        out_shape=jax.ShapeDtypeStruct((b * s, qw), BF16),
        compiler_params=_cparams(("arbitrary", "arbitrary")),
        name="gqa_lat",
    )(sink, p, *([p] * 8))


def _merge_kernel(x_ref, yr_ref, ydl_ref, ydc_ref, ygl_ref, ygc_ref, gmix_ref, gffn_ref, sh1_ref, sc1_ref, ga1_ref,
                  sh2_ref, sc2_ref, wg_ref, wb_ref, wo_ref, wrh_ref, wrl_ref, br_ref,
                  x1_ref, h2_ref, ti_ref, tw_ref, *, n_lat_tiles):
    d = x_ref.shape[1]
    x = x_ref[...]
    h = (_rms(x, gmix_ref[...]) * (1.0 + sc1_ref[0]) + sh1_ref[0]).astype(BF16)
    rw = RET_HEADS * RET_DV
    dw = DIFF_HEADS * LANES
    is_lat = pl.program_id(0) < n_lat_tiles
    yd = jnp.where(is_lat, ydl_ref[...], ydc_ref[...])
    yg = jnp.where(is_lat, ygl_ref[...], ygc_ref[...])
    merged = None
    for bi, (y, r0, r1) in enumerate(((yr_ref[...], 0, rw), (yd, rw, rw + dw), (yg, rw + dw, wb_ref.shape[0]))):
        gate = _sigmoid(_dot(h, wg_ref[:, bi * d:(bi + 1) * d]))
        term = gate * _dot(y, wb_ref[r0:r1, :])
        merged = term if merged is None else merged + term
    x1 = x + ga1_ref[0] * _dot(merged.astype(BF16), wo_ref[...])
    x1_ref[...] = x1
    h2 = _rms(x1, gffn_ref[...]) * (1.0 + sc2_ref[0]) + sh2_ref[0]
    tm, ns = x_ref.shape[0], d // LANES
    for s in range(ns):
        h2_ref[pl.ds(s, tm, stride=ns), :] = h2[:, s * LANES:(s + 1) * LANES]
    hh, hl = _split_bf16(h2)
    wrh, wrl = wrh_ref[...], wrl_ref[...]
    logits = _dot(hh, wrh) + (_dot(hl, wrh) + _dot(hh, wrl)) + br_ref[...]
    eidx = lax.broadcasted_iota(jnp.int32, logits.shape, 1).astype(F32)
    vals = []
    for k in range(TOP_K):
        mx = jnp.max(logits, axis=1, keepdims=True)
        ix = jnp.min(jnp.where(logits == mx, eidx, float(N_EXPERTS)), axis=1, keepdims=True)
        vals.append(mx)
        ti_ref[:, k:k + 1] = ix.astype(jnp.int32)
        logits = jnp.where(eidx == ix, -jnp.inf, logits)
    es = [jnp.exp(v - vals[0]) for v in vals]
    tot = es[0] + es[1] + es[2] + es[3]
    for k in range(TOP_K):
        tw_ref[:, k:k + 1] = es[k] / tot


def _merge(xall, yr, yd, yg, gmix, gffn, mod3, wg, wb, wo, wrh, wrl, br, dims, ntok):
    d = xall.shape[1]
    tm = dims["tm_merge"]
    tps = dims["seq"] // tm
    n_lat_tiles = dims["batch"] * tps

    def mrow(i):
        return jnp.minimum(i // tps, dims["batch"])

    def modspec(k):
        return pl.BlockSpec((1, 1, d), lambda i: (mrow(i), 0, k))

    def full(a):
        return pl.BlockSpec(a.shape, lambda i: (0,) * a.ndim)

    def rows(w):
        return pl.BlockSpec((tm, w), lambda i: (i, 0))

    def lat_ctx(pair):
        lat, ctx = pair
        w = lat.shape[1]
        lat_spec = pl.BlockSpec((tm, w), lambda i: (jnp.minimum(i, n_lat_tiles - 1), 0))
        if ctx is None:
            return [lat_spec, lat_spec], [lat, lat]
        n_ctx_tiles = ctx.shape[0] // tm
        ctx_spec = pl.BlockSpec((tm, w), lambda i: (jnp.clip(i - n_lat_tiles, 0, n_ctx_tiles - 1), 0))
        return [lat_spec, ctx_spec], [lat, ctx]

    yd_specs, yd_args = lat_ctx(yd)
    yg_specs, yg_args = lat_ctx(yg)
    return pl.pallas_call(
        functools.partial(_merge_kernel, n_lat_tiles=n_lat_tiles),
        grid=(ntok // tm,),
        in_specs=[rows(d), rows(yr.shape[1])] + yd_specs + yg_specs + [full(gmix), full(gffn),
                  modspec(0), modspec(1), modspec(2), modspec(3), modspec(4),
                  full(wg), full(wb), full(wo), full(wrh), full(wrl), full(br)],
        out_specs=[rows(d), pl.BlockSpec((tm * (d // LANES), LANES), lambda i: (i, 0)),
                   pl.BlockSpec((tm, TOP_K), lambda i: (i, 0)), pl.BlockSpec((tm, TOP_K), lambda i: (i, 0))],
        out_shape=[jax.ShapeDtypeStruct((ntok, d), F32), jax.ShapeDtypeStruct((ntok * (d // LANES), LANES), F32),
                   jax.ShapeDtypeStruct((ntok, TOP_K), jnp.int32), jax.ShapeDtypeStruct((ntok, TOP_K), F32)],
        compiler_params=_cparams(("arbitrary",)),
        name="merge",
    )(xall, yr, *yd_args, *yg_args, gmix, gffn, mod3, mod3, mod3, mod3, mod3, wg, wb, wo, wrh, wrl, br)


TOK_BITS = 15
FFN_CHUNK = 256
GATHER_DEPTH = 3


def _moe_kernel(be_ref, nv_ref, nu_ref, base_ref, code_ref, h2_hbm, wgu_ref, bgu_ref, wdn_ref, bdn_ref, o_hbm,
                xbuf, ybuf, wgu_bf, wdn_bf, gsem, ssem):
    i = pl.program_id(0)
    nu = nu_ref[0]
    slot = i % 2
    gslot = lax.rem(i, GATHER_DEPTH)
    last_blk = pl.num_programs(0) - 1
    nb = MOE_BLOCK
    d = wdn_bf.shape[0]
    ns = d // LANES

    def gather_rows(blk, sl, r0, r1):
        base = base_ref[blk]
        for r in range(r0, r1):
            tok = code_ref[base + r] & ((1 << TOK_BITS) - 1)
            pltpu.make_async_copy(h2_hbm.at[pl.ds(pl.multiple_of(tok * ns, ns), ns)], xbuf.at[sl, pl.ds(r * ns, ns)],
                                  gsem.at[sl]).start(priority=r % 2)

    def for_rows(n, body):
        @pl.when(n == nb)
        def _():
            for r in range(nb):
                body(r)

        @pl.when(n != nb)
        def _():
            def step(r, carry):
                body(r)
                return carry
            lax.fori_loop(0, n, step, 0)

    def wait_rows(n, copy_of_rows):
        @pl.when(n == nb)
        def _():
            copy_of_rows(nb).wait()

        @pl.when(n != nb)
        def _():
            def body(r, carry):
                copy_of_rows(1).wait()
                return carry
            lax.fori_loop(0, n, body, 0)

    def gather_wait(sl):
        pltpu.make_async_copy(h2_hbm.at[pl.ds(0, nb * ns)], xbuf.at[sl], gsem.at[sl]).wait()

    def scatter_start(blk, sl):
        base = base_ref[blk]

        def body(r):
            dst = lax.shift_right_logical(code_ref[base + r], TOK_BITS)
            pltpu.make_async_copy(ybuf.at[sl, pl.ds(pl.multiple_of(r * ns, ns), ns)],
                                  o_hbm.at[pl.ds(pl.multiple_of(dst * ns, ns), ns)], ssem.at[sl]
                                  ).start(priority=r % 2 if isinstance(r, int) else 0)
        for_rows(nv_ref[blk], body)

    def scatter_wait(blk, sl):
        wait_rows(nv_ref[blk], lambda n: pltpu.make_async_copy(
            ybuf.at[sl, pl.ds(0, n * ns)], o_hbm.at[pl.ds(0, n * ns)], ssem.at[sl]))

    @pl.when(i < nu)
    def _():
        @pl.when(i == 0)
        def _():
            gather_rows(0, 0, 0, nb)
            gather_rows(jnp.minimum(1, last_blk), 1, 0, nb)

        @pl.when((i == 0) | (be_ref[i] != be_ref[jnp.maximum(i - 1, 0)]))
        def _():
            rc = 128

            def cast(r, carry):
                r0 = pl.multiple_of(r * rc, rc)
                wgu_bf[pl.ds(r0, rc), :] = wgu_ref[0, 0, pl.ds(r0, rc), :].astype(BF16)
                wdn_bf[pl.ds(r0, rc), :] = wdn_ref[0, 0, pl.ds(r0, rc), :].astype(BF16)
                return carry
            lax.fori_loop(0, d // rc, cast, 0)

        gather_wait(gslot)
        nxt = jnp.minimum(i + GATHER_DEPTH - 1, last_blk)
        nslot = lax.rem(i + GATHER_DEPTH - 1, GATHER_DEPTH)
        x = jnp.concatenate([xbuf[gslot, pl.ds(s, nb, stride=ns), :] for s in range(ns)], axis=1).astype(BF16)
        cw = FFN_CHUNK
        nch = d // cw
        y = None
        for ci in range(nch):
            c0, c1 = ci * cw, (ci + 1) * cw
            gather_rows(nxt, nslot, ci * (nb // nch), (ci + 1) * (nb // nch))
            glu = jnp.minimum(_dot(x, wgu_bf[:, c0:c1]) + bgu_ref[0, 0, :, c0:c1], SWIGLU_LIMIT)
            lin = jnp.clip(_dot(x, wgu_bf[:, d + c0:d + c1]) + bgu_ref[0, 0, :, d + c0:d + c1],
                           -SWIGLU_LIMIT, SWIGLU_LIMIT)
            act = glu * _sigmoid(SWIGLU_ALPHA * glu) * (lin + 1.0)
            part = _dot(act.astype(BF16), wdn_bf[c0:c1, :])
            y = part if y is None else y + part
        y = y + bdn_ref[0, 0]

        @pl.when(i >= 2)
        def _():
            scatter_wait(i - 2, slot)

        for s in range(ns):
            ybuf[slot, pl.ds(s, nb, stride=ns), :] = y[:, s * LANES:(s + 1) * LANES]
        scatter_start(i, slot)

        @pl.when(i == nu - 1)
        def _():
            gather_wait(lax.rem(i + 1, GATHER_DEPTH))
            gather_wait(nslot)

            @pl.when(i >= 1)
            def _():
                scatter_wait(i - 1, 1 - slot)
            scatter_wait(i, slot)


def _moe(h2, route, w_gu, b_gu, w_dn, b_dn, layer):
    codes, block_expert, n_valid, n_used, base = route
    d = w_dn.shape[-1]
    ns = d // LANES
    ntok = h2.shape[0] // ns
    n_blocks = block_expert.shape[0]
    depth, ne = w_gu.shape[:2]

    def wspec(rows, cols):
        return pl.BlockSpec((1, 1, rows, cols), lambda i, be, nv, nu, bs, cd: (layer, be[i], 0, 0))

    grid_spec = pltpu.PrefetchScalarGridSpec(
        num_scalar_prefetch=5,
        grid=(n_blocks,),
        in_specs=[pl.BlockSpec(memory_space=pl.ANY),
                  wspec(d, 2 * d), wspec(1, 2 * d), wspec(d, d), wspec(1, d)],
        out_specs=pl.BlockSpec(memory_space=pl.ANY),
        scratch_shapes=[pltpu.VMEM((GATHER_DEPTH, MOE_BLOCK * ns, LANES), F32),
                        pltpu.VMEM((2, MOE_BLOCK * ns, LANES), F32),
                        pltpu.VMEM((d, 2 * d), BF16), pltpu.VMEM((d, d), BF16),
                        pltpu.SemaphoreType.DMA((GATHER_DEPTH,)), pltpu.SemaphoreType.DMA((2,))])
    return pl.pallas_call(
        _moe_kernel,
        grid_spec=grid_spec,
        out_shape=jax.ShapeDtypeStruct((TOP_K * ntok * ns, LANES), F32),
        compiler_params=_cparams(("arbitrary",)),
        name="moe",
    )(block_expert, n_valid, n_used, base, codes, h2, w_gu, b_gu.reshape(depth, ne, 1, 2 * d), w_dn,
      b_dn.reshape(depth, ne, 1, d))


def _route(ti, ntok):
    a = TOP_K * ntok
    n_blocks = (a + N_EXPERTS * (MOE_BLOCK - 1) + MOE_BLOCK - 1) // MOE_BLOCK
    expert = ti.T.reshape(a)
    aid_bits = (a - 1).bit_length()
    assert aid_bits + 5 < 32 and N_EXPERTS <= 32
    aid = jnp.arange(a, dtype=jnp.int32)
    aid_sorted = jnp.sort(lax.shift_left(expert, aid_bits) | aid) & ((1 << aid_bits) - 1)
    codes = lax.shift_left(aid_sorted, TOK_BITS) | (aid_sorted % ntok)
    codes = jnp.concatenate([codes, jnp.zeros((MOE_BLOCK,), jnp.int32)])
    counts = jnp.sum((expert[:, None] == jnp.arange(N_EXPERTS, dtype=jnp.int32)[None, :]).astype(jnp.int32), axis=0)
    padded = (counts + MOE_BLOCK - 1) // MOE_BLOCK * MOE_BLOCK
    pad_end = jnp.cumsum(padded)
    pad_start = pad_end - padded
    cstart = jnp.cumsum(counts) - counts
    n_used = (pad_end[-1:] // MOE_BLOCK).astype(jnp.int32)
    block_start = jnp.arange(n_blocks, dtype=jnp.int32) * MOE_BLOCK
    block_expert = jnp.minimum(jnp.sum((pad_end[None, :] <= block_start[:, None]).astype(jnp.int32), axis=1),
                               N_EXPERTS - 1)
    n_valid = jnp.clip(counts[block_expert] - (block_start - pad_start[block_expert]), 0, MOE_BLOCK).astype(jnp.int32)
    base = jnp.clip(block_start - (pad_start - cstart)[block_expert], 0, a).astype(jnp.int32)
    return codes, block_expert, n_valid, n_used, base


def _combine_kernel(x1_ref, o0_ref, o1_ref, o2_ref, o3_ref, tw_ref, ga2_ref, gf_ref, x2_ref, *, final):
    o_refs = (o0_ref, o1_ref, o2_ref, o3_ref)
    tm, ns = x1_ref.shape[0], x1_ref.shape[1] // LANES
    w = [jnp.broadcast_to(tw_ref[:, k:k + 1], (tm, LANES)) for k in range(TOP_K)]
    for s in range(ns):
        cols = slice(s * LANES, (s + 1) * LANES)
        y = None
        for k in range(TOP_K):
            term = w[k] * o_refs[k][pl.ds(s, tm, stride=ns), :]
            y = term if y is None else y + term
        x2_ref[:, cols] = x1_ref[:, cols] + ga2_ref[0, :, cols] * y
    if final:
        x2_ref[...] = _rms(x2_ref[...], gf_ref[...])


def _combine(x1, o4, tw, mod3, g_final, dims, final):
    ntok, d = x1.shape
    tm = dims["tm_merge"]
    tps = dims["seq"] // tm
    nt = ntok // tm

    def ospec(k):
        return pl.BlockSpec((tm * (d // LANES), LANES), lambda i: (k * nt + i, 0))

    return pl.pallas_call(
        functools.partial(_combine_kernel, final=final),
        grid=(nt,),
        in_specs=[pl.BlockSpec((tm, d), lambda i: (i, 0)), ospec(0), ospec(1), ospec(2), ospec(3),
                  pl.BlockSpec((tm, TOP_K), lambda i: (i, 0)),
                  pl.BlockSpec((1, 1, d), lambda i: (jnp.minimum(i // tps, dims["batch"]), 0, 5)),
                  pl.BlockSpec((1, d), lambda i: (0, 0))],
        out_specs=pl.BlockSpec((tm, d), lambda i: (i, 0)),
        out_shape=jax.ShapeDtypeStruct((ntok, d), F32),
        compiler_params=_cparams(("arbitrary",)),
        name="combine",
    )(x1, o4, o4, o4, o4, tw, mod3, g_final)


def _rope_tables(seq, pad_rows):
    pos = jnp.arange(seq, dtype=jnp.int32)
    row = (pos // GRID_W).astype(F32)
    col = (pos % GRID_W).astype(F32)

    def cs(head_dim):
        quarter = head_dim // 4
        inv_freq = ROPE_BASE ** (-jnp.arange(quarter, dtype=F32) / quarter)
        ang = jnp.concatenate([row[:, None] * inv_freq, col[:, None] * inv_freq], axis=-1)
        return jnp.cos(ang), jnp.sin(ang)

    c128, s128 = cs(RET_DK)
    c64, s64 = cs(DIFF_DK)
    ones = jnp.ones((pad_rows, LANES), F32)
    zeros = jnp.zeros((pad_rows, LANES), F32)
    a128 = jnp.concatenate([jnp.concatenate([c128, c128], -1), ones], 0)
    b128 = jnp.concatenate([jnp.concatenate([-s128, s128], -1), zeros], 0)
    a64 = jnp.concatenate([jnp.concatenate([c64, c64, c64, c64], -1), ones], 0)
    b64 = jnp.concatenate([jnp.concatenate([-s64, s64, -s64, s64], -1), zeros], 0)
    return a128, b128, a64, b64


def _pick_tile(cands, *sizes):
    for t in cands:
        if all(s % t == 0 for s in sizes):
            return t
    raise ValueError("no tile size fits")


def kernel(x, c, ctx, c_ctx, w_mod, b_mod, g_norm_mix, g_norm_ffn, w_in, ret_decay_fwd, ret_decay_bwd, diff_lambda_q1, diff_lambda_k1, diff_lambda_q2, diff_lambda_k2, diff_subln_g, gqa_sink, w_branch, w_out, w_router, b_router, w_gate_up, b_gate_up, w_down, b_down, g_final):
    b, s, d = x.shape
    n_ctx = ctx.shape[1]
    depth = w_mod.shape[0]
    n_lat = b * s
    tt = n_lat + b * n_ctx
    assert s % RET_CHUNK == 0 and n_ctx % RET_CHUNK == 0 and s % GQA_TQ == 0 and n_lat % n_ctx == 0
    assert tt < (1 << TOK_BITS) and b < 8
    tm = _pick_tile((1024, 512, 256), s, b * n_ctx)
    tm_merge = _pick_tile((512, 256), s, b * n_ctx)
    dims = dict(batch=b, seq=s, ctx=n_ctx, tm=tm, tm_merge=tm_merge,
                n_lat_tiles=n_lat // tm, tiles_per_seq=s // tm)

    xall = jnp.concatenate([x.reshape(n_lat, d), ctx.reshape(b * n_ctx, d)], axis=0)
    cvec = jnp.concatenate([c, c_ctx[None, :], jnp.zeros((8 - b - 1, d), F32)], axis=0)
    mod = _mod_all(cvec, w_mod, b_mod)
    tabs = _rope_tables(s, tm)

    for l in range(depth):
        last = l == depth - 1
        mod3 = mod[l].reshape(8, 1, 6 * d)
        w_l = w_in[l]
        p = _inproj(xall, g_norm_mix[l][None, :], mod3, w_in, l, tabs, dims)

        lg = jnp.stack([jax.nn.log_sigmoid(ret_decay_fwd[l].astype(F32)),
                        jax.nn.log_sigmoid(ret_decay_bwd[l].astype(F32))], axis=0)
        y_r = _retention(p, lg, dims)

        lam_init = 0.8 - 0.6 * math.exp(-0.3 * l)
        lam = (jnp.exp(jnp.sum(diff_lambda_q1[l].astype(F32) * diff_lambda_k1[l].astype(F32)))
               - jnp.exp(jnp.sum(diff_lambda_q2[l].astype(F32) * diff_lambda_k2[l].astype(F32))) + lam_init)
        scal = jnp.stack([lam, jnp.asarray(1.0 - lam_init, F32)])
        subln = diff_subln_g[l][None, :].astype(F32)
        sink = gqa_sink[l].astype(F32)
        y_d = (_diff_attention(p, scal, subln, dims, False),
               None if last else _diff_attention(p, scal, subln, dims, True))
        y_g = (_gqa(p, sink, dims, False), None if last else _gqa(p, sink, dims, True))
        ntok = n_lat if last else tt

        wrh, wrl = _split_bf16(w_router[l])
        x1, h2, ti, tw = _merge(xall, y_r, y_d, y_g, g_norm_mix[l][None, :], g_norm_ffn[l][None, :], mod3,
                                w_l[:, P_COLS:].astype(BF16), w_branch[l].astype(BF16), w_out[l].astype(BF16),
                                wrh, wrl, b_router[l][None, :].astype(F32), dims, ntok)
        o4 = _moe(h2, _route(ti, ntok), w_gate_up, b_gate_up, w_down, b_down, l)
        xall = _combine(x1, o4, tw, mod3, g_final[None, :], dims, last)
    return xall.reshape(b, s, d)
```

```python
import functools
import math

import jax
import jax.numpy as jnp
from jax import lax
from jax.experimental import pallas as pl
from jax.experimental.pallas import tpu as pltpu

F32 = jnp.float32
BF16 = jnp.bfloat16

GRID_W = 64
RET_HEADS, RET_DK, RET_DV = 4, 128, 256
DIFF_HEADS, DIFF_DK = 4, 64
GQA_HEADS, GQA_KV_HEADS, GQA_DH = 8, 2, 64
WINDOW = 128
N_EXPERTS, TOP_K = 32, 4
SWIGLU_ALPHA, SWIGLU_LIMIT = 1.702, 7.0
MOE_BLOCK = 256
ROPE_BASE = 10000.0
NORM_EPS = 1e-6
NEG_BIG = -1e30

LANES = 128
P_COLS = 5376
IN_TN = 768
RET_CHUNK = 256
DIFF_TQ, DIFF_TK = 512, 512
DIFF_ROW_CHUNK = 64
LOG2E = math.log2(math.e)
GQA_TQ = 256
VMEM_LIMIT = 56 * 1024 * 1024

_PLAIN = ("plain", 1.0)
_GROUP_KINDS = ([("r128", RET_DK ** -0.5)] * 4 + [("r128", 1.0)] * 4 + [_PLAIN] * 16
                + [("r64", DIFF_DK ** -0.5 * LOG2E)] * 4 + [("r64", 1.0)] * 4 + [_PLAIN] * 4
                + [("r64", GQA_DH ** -0.5)] * 4 + [("r64", 1.0)] + [_PLAIN])
COL_RET_Q, COL_RET_K, COL_RET_V, COL_RET_G = 0, 512, 1024, 2048
COL_DIFF_Q, COL_DIFF_K, COL_DIFF_V = 3072, 3584, 4096
COL_GQA_Q, COL_GQA_K, COL_GQA_V = 4608, 5120, 5248


def _cparams(sem):
    return pltpu.CompilerParams(dimension_semantics=sem, vmem_limit_bytes=VMEM_LIMIT)


def _sigmoid(x):
    return 1.0 / (1.0 + jnp.exp(-x))


def _dot(a, b):
    return jnp.dot(a, b, preferred_element_type=F32)


def _dot_nt(a, b):
    return lax.dot_general(a, b, (((1,), (1,)), ((), ())), preferred_element_type=F32)


def _dot_tn(a, b):
    return lax.dot_general(a, b, (((0,), (0,)), ((), ())), preferred_element_type=F32)


def _split_bf16(a):
    hi = a.astype(BF16)
    lo = (a - hi.astype(F32)).astype(BF16)
    return hi, lo


def _rms(x, g):
    return x * lax.rsqrt(jnp.mean(x * x, axis=-1, keepdims=True) + NORM_EPS) * g


def _mod_kernel(c_ref, w_ref, b_ref, o_ref):
    c = c_ref[...]
    s = c * _sigmoid(c)
    sh, sl = _split_bf16(s)
    wh, wl = _split_bf16(w_ref[0])
    o_ref[0] = _dot(sh, wh) + (_dot(sl, wh) + _dot(sh, wl)) + b_ref[0]


def _mod_all(cvec, w_mod, b_mod):
    depth, d, n = w_mod.shape
    tn = n // 4
    return pl.pallas_call(
        _mod_kernel,
        grid=(depth, n // tn),
        in_specs=[pl.BlockSpec((8, d), lambda l, j: (0, 0)),
                  pl.BlockSpec((1, d, tn), lambda l, j: (l, 0, j)),
                  pl.BlockSpec((1, 1, tn), lambda l, j: (l, 0, j))],
        out_specs=pl.BlockSpec((1, 8, tn), lambda l, j: (l, 0, j)),
        out_shape=jax.ShapeDtypeStruct((depth, 8, n), F32),
        compiler_params=_cparams(("arbitrary", "arbitrary")),
        name="mod",
    )(cvec, w_mod, b_mod.reshape(depth, 1, n))


def _rope_group(a, kind, scale, a128, b128, a64, b64):
    if kind == "plain":
        return a
    if kind == "r128":
        out = a * a128 + pltpu.roll(a, 64, 1) * b128
    else:
        lane = lax.broadcasted_iota(jnp.int32, (1, LANES), 1)
        first = (lane & 63) < 32
        partner = jnp.where(first, pltpu.roll(a, 96, 1), pltpu.roll(a, 32, 1))
        out = a * a64 + partner * b64
    return out * scale if scale != 1.0 else out


def _inproj_kernel(x_ref, g_ref, sc_ref, sh_ref, w_ref, a128_ref, b128_ref, a64_ref, b64_ref,
                   o_ref, h_scr, acc_scr, *, n_tiles):
    j = pl.program_id(1)

    @pl.when(j == 0)
    def _():
        y = _rms(x_ref[...], g_ref[...])
        h_scr[...] = (y * (1.0 + sc_ref[0]) + sh_ref[0]).astype(BF16)

    acc_scr[...] = _dot(h_scr[...], w_ref[0].astype(BF16))
    gpt = IN_TN // LANES
    for jj in range(n_tiles):
        @pl.when(j == jj)
        def _(jj=jj):
            for g in range(gpt):
                kind, scale = _GROUP_KINDS[jj * gpt + g]
                a = acc_scr[:, g * LANES:(g + 1) * LANES]
                out = _rope_group(a, kind, scale, a128_ref[...], b128_ref[...], a64_ref[...], b64_ref[...])
                o_ref[:, g * LANES:(g + 1) * LANES] = out.astype(BF16)


def _inproj(xall, g, mod3, w_in, layer, tabs, dims):
    tt, d = xall.shape
    tm = dims["tm"]
    n_lat_tiles, tps = dims["n_lat_tiles"], dims["tiles_per_seq"]
    n_tiles = P_COLS // IN_TN

    def mrow(i):
        return jnp.minimum(i // tps, dims["batch"])

    def trow(i):
        return jnp.where(i < n_lat_tiles, i % tps, tps)

    tab_spec = pl.BlockSpec((tm, LANES), lambda i, j: (trow(i), 0))
    return pl.pallas_call(
        functools.partial(_inproj_kernel, n_tiles=n_tiles),
        grid=(tt // tm, n_tiles),
        in_specs=[pl.BlockSpec((tm, d), lambda i, j: (i, 0)),
                  pl.BlockSpec((1, d), lambda i, j: (0, 0)),
                  pl.BlockSpec((1, 1, d), lambda i, j: (mrow(i), 0, 1)),
                  pl.BlockSpec((1, 1, d), lambda i, j: (mrow(i), 0, 0)),
                  pl.BlockSpec((1, d, IN_TN), lambda i, j: (layer, 0, j)),
                  tab_spec, tab_spec, tab_spec, tab_spec],
        out_specs=pl.BlockSpec((tm, IN_TN), lambda i, j: (i, j)),
        out_shape=jax.ShapeDtypeStruct((tt, P_COLS), BF16),
        scratch_shapes=[pltpu.VMEM((tm, d), BF16), pltpu.VMEM((tm, IN_TN), F32)],
        compiler_params=_cparams(("arbitrary", "arbitrary")),
        name="inproj",
    )(xall, g, mod3, mod3, w_in, *tabs)


def _ret_kernel(lg_ref, q_ref, k_ref, v_ref, g_ref, y_ref, s_scr, of_scr, *, ncc, ncl):
    ph = pl.program_id(1)
    t = pl.program_id(2)
    ch = RET_CHUNK

    @pl.when(t == 0)
    def _():
        s_scr[...] = jnp.zeros_like(s_scr)

    fwd = ph == 0
    ii = lax.broadcasted_iota(jnp.int32, (ch, ch), 0).astype(F32)
    jj = lax.broadcasted_iota(jnp.int32, (ch, ch), 1).astype(F32)
    rel = jnp.where(fwd, ii - jj, jj - ii)
    pos = lax.broadcasted_iota(jnp.int32, (ch, 1), 0).astype(F32)
    qpos = jnp.where(fwd, pos + 1.0, ch - pos)
    kpos = jnp.where(fwd, ch - 1.0 - pos, pos)
    mirror = jnp.where(t < ncc, ncc - 1 - t, ncc + ncl - 1 - (t - ncc))
    row0 = pl.multiple_of(jnp.where(fwd, t, mirror) * ch, ch)

    for h in range(RET_HEADS):
        lg = lg_ref[ph, h]
        dmask = jnp.where(rel >= 0.0, jnp.exp(jnp.maximum(rel, 0.0) * lg), 0.0)
        q = q_ref[:, h * RET_DK:(h + 1) * RET_DK]
        k = k_ref[:, h * RET_DK:(h + 1) * RET_DK]
        v = v_ref[:, h * RET_DV:(h + 1) * RET_DV]
        a = (_dot_nt(q, k) * dmask).astype(BF16)
        qd = (q.astype(F32) * jnp.exp(qpos * lg)).astype(BF16)
        kd = (k.astype(F32) * jnp.exp(kpos * lg)).astype(BF16)
        state = s_scr[h]
        o = _dot(a, v) + _dot(qd, state.astype(BF16))
        s_scr[h] = state * jnp.exp(ch * lg) + _dot_tn(kd, v)
        cs = slice(h * RET_DV, (h + 1) * RET_DV)

        @pl.when(fwd)
        def _(o=o, cs=cs):
            of_scr[pl.ds(row0, ch), cs] = o

        @pl.when(jnp.logical_not(fwd))
        def _(o=o, cs=cs):
            ot = o + of_scr[pl.ds(row0, ch), cs]
            mu = jnp.mean(ot, axis=-1, keepdims=True)
            oc = ot - mu
            var = jnp.mean(oc * oc, axis=-1, keepdims=True)
            on = oc * lax.rsqrt(var + NORM_EPS)
            gate = g_ref[:, cs].astype(F32)
            y_ref[:, cs] = (gate * _sigmoid(gate) * on).astype(BF16)


def _retention(p, lg, dims):
    tt = p.shape[0]
    ch = RET_CHUNK
    b, s, c = dims["batch"], dims["seq"], dims["ctx"]
    ncc, ncl = c // ch, s // ch
    nsteps = ncc + ncl
    ctx_base = (b * s) // ch

    def rblk(bi, ph, t):
        fwd_blk = jnp.where(t < ncc, ctx_base + bi * ncc + t, bi * ncl + (t - ncc))
        bwd_blk = jnp.where(t < ncc, ctx_base + bi * ncc + (ncc - 1 - t), bi * ncl + (ncl - 1 - (t - ncc)))
        return jnp.where(ph == 0, fwd_blk, bwd_blk)

    def oblk(bi, ph, t):
        return rblk(bi, 1, jnp.where(ph == 0, 0, t))

    qk_w = RET_HEADS * RET_DK
    v_w = RET_HEADS * RET_DV
    return pl.pallas_call(
        functools.partial(_ret_kernel, ncc=ncc, ncl=ncl),
        grid=(b, 2, nsteps),
        in_specs=[pl.BlockSpec(memory_space=pltpu.SMEM),
                  pl.BlockSpec((ch, qk_w), lambda bi, ph, t: (rblk(bi, ph, t), COL_RET_Q // qk_w)),
                  pl.BlockSpec((ch, qk_w), lambda bi, ph, t: (rblk(bi, ph, t), COL_RET_K // qk_w)),
                  pl.BlockSpec((ch, v_w), lambda bi, ph, t: (rblk(bi, ph, t), COL_RET_V // v_w)),
                  pl.BlockSpec((ch, v_w), lambda bi, ph, t: (rblk(bi, ph, t), COL_RET_G // v_w))],
        out_specs=pl.BlockSpec((ch, v_w), lambda bi, ph, t: (oblk(bi, ph, t), 0)),
        out_shape=jax.ShapeDtypeStruct((tt, v_w), BF16),
        scratch_shapes=[pltpu.VMEM((RET_HEADS, RET_DK, RET_DV), F32),
                        pltpu.VMEM((nsteps * ch, v_w), F32)],
        compiler_params=_cparams(("arbitrary", "arbitrary", "arbitrary")),
        name="retention",
    )(lg, p, p, p, p)


def _diff_kernel(sc_ref, q_ref, *refs, n_src, tk):
    kv = refs[:2 * n_src]
    g_ref, o_ref = refs[2 * n_src], refs[2 * n_src + 1]
    kx_scr, vx_scr = refs[2 * n_src + 2], refs[2 * n_src + 3]
    nbuf = 8
    streams = (refs[2 * n_src + 4:2 * n_src + 4 + nbuf], refs[2 * n_src + 4 + nbuf:2 * n_src + 4 + 2 * nbuf])
    tq = q_ref.shape[0]
    rc = DIFF_ROW_CHUNK
    n_keys = sum(kv[2 * si].shape[0] for si in range(n_src))
    n_tiles = kx_scr.shape[0] // tk

    @pl.when(pl.program_id(2) == 0)
    def _():
        vx_scr[:, LANES:] = jnp.ones((vx_scr.shape[0], LANES), BF16)
        r0 = 0
        for si in range(n_src):
            k_ref, v_ref = kv[2 * si], kv[2 * si + 1]
            kx_scr[r0:r0 + k_ref.shape[0], :] = k_ref[...]
            vx_scr[r0:r0 + v_ref.shape[0], :LANES] = v_ref[...]
            r0 += k_ref.shape[0]
        if r0 < kx_scr.shape[0]:
            kx_scr[r0:, :] = jnp.zeros((kx_scr.shape[0] - r0, LANES), BF16)
            vx_scr[r0:, :LANES] = jnp.zeros((kx_scr.shape[0] - r0, LANES), BF16)

    q = q_ref[...]
    lane = lax.broadcasted_iota(jnp.int32, (1, LANES), 1)
    zero = jnp.zeros_like(q)
    qs = (jnp.where(lane < DIFF_DK, q, zero), jnp.where(lane >= DIFF_DK, q, zero))
    S_BUF, P_BUF, AL_BUF, ACC, MAX = 0, 2, 4, 6, 7
    for bufs in streams:
        bufs[ACC][...] = jnp.zeros_like(bufs[ACC])
        bufs[MAX][...] = jnp.full(bufs[MAX].shape, NEG_BIG, F32)

    def tile_rows(t):
        return pl.ds(t * tk if isinstance(t, int) else pl.multiple_of(t * tk, tk), tk)

    def scores(t, par):
        kt = kx_scr[tile_rows(t), :]
        for st, bufs in enumerate(streams):
            bufs[S_BUF + par][...] = _dot_nt(qs[st], kt)

    def accumulate(t, par):
        vt = vx_scr[tile_rows(t), :]
        for bufs in streams:
            bufs[ACC][...] = bufs[AL_BUF + par][...] * bufs[ACC][...] + _dot(bufs[P_BUF + par][...], vt)

    def stage(t, par, prefetch, n_valid, drain_prev):
        if prefetch:
            scores(t + 1, 1 - par)
        if drain_prev:
            accumulate(t - 1, 1 - par)
        for bufs in streams:
            s_scr, p_scr, al_scr, m_scr = bufs[S_BUF + par], bufs[P_BUF + par], bufs[AL_BUF + par], bufs[MAX]
            m_all = m_scr[...]
            m_new, alpha = [], []
            for r in range(tq // rc):
                rows = slice(r * rc, (r + 1) * rc)
                s = s_scr[rows, :]
                if n_valid < tk:
                    col = lax.broadcasted_iota(jnp.int32, s.shape, 1)
                    s = jnp.where(col < n_valid, s, NEG_BIG)
                m_old = m_all[rows]
                mn = jnp.maximum(m_old, jnp.max(s, axis=-1, keepdims=True))
                p_scr[rows, :] = jnp.exp2(s - mn).astype(BF16)
                alpha.append(jnp.exp2(m_old - mn))
                m_new.append(mn)
            m_scr[...] = jnp.concatenate(m_new, axis=0)
            al_scr[...] = jnp.concatenate(alpha, axis=0)

    def n_valid_of(t):
        return n_keys - t * tk if t == n_tiles - 1 else tk

    scores(0, 0)
    stage(0, 0, n_tiles > 1, n_valid_of(0), False)
    pairs = max(0, (n_tiles - 2) // 2)
    if pairs > 0:
        def body(u, carry):
            stage(2 * u + 1, 1, True, tk, True)
            stage(2 * u + 2, 0, True, tk, True)
            return carry
        lax.fori_loop(0, pairs, body, 0)
    for t in range(2 * pairs + 1, n_tiles):
        stage(t, t % 2, t < n_tiles - 1, n_valid_of(t), True)
    accumulate(n_tiles - 1, (n_tiles - 1) % 2)
    lam = sc_ref[0]
    a1, a2 = streams[0][ACC], streams[1][ACC]
    o = a1[:, :LANES] / a1[:, LANES:] - lam * (a2[:, :LANES] / a2[:, LANES:])
    o = o * lax.rsqrt(jnp.mean(o * o, axis=-1, keepdims=True) + NORM_EPS) * g_ref[...] * sc_ref[1]
    o_ref[...] = o.astype(BF16)


def _diff_attention(p, scal, subln_g, dims, ctx_queries):
    b, s, c = dims["batch"], dims["seq"], dims["ctx"]
    tt = p.shape[0]
    qc0, kc0, vc0 = COL_DIFF_Q // LANES, COL_DIFF_K // LANES, COL_DIFF_V // LANES
    ctx_blk = (b * s) // c
    kctx = pl.BlockSpec((c, LANES), lambda bi, h, qi: (ctx_blk + bi, kc0 + h))
    vctx = pl.BlockSpec((c, LANES), lambda bi, h, qi: (ctx_blk + bi, vc0 + h))
    if ctx_queries:
        tq, nq = c, 1
        q_spec = pl.BlockSpec((tq, LANES), lambda bi, h, qi: (ctx_blk + bi, qc0 + h))
        o_spec = pl.BlockSpec((tq, LANES), lambda bi, h, qi: (bi, h))
        out_rows = b * c
        kv_specs, n_src, kv_args = [kctx, vctx], 1, (p, p)
    else:
        tq = min(DIFF_TQ, s)
        nq = s // tq
        q_spec = pl.BlockSpec((tq, LANES), lambda bi, h, qi: (bi * nq + qi, qc0 + h))
        o_spec = pl.BlockSpec((tq, LANES), lambda bi, h, qi: (bi * nq + qi, h))
        out_rows = b * s
        klat = pl.BlockSpec((s, LANES), lambda bi, h, qi: (bi, kc0 + h))
        vlat = pl.BlockSpec((s, LANES), lambda bi, h, qi: (bi, vc0 + h))
        kv_specs, n_src, kv_args = [klat, vlat, kctx, vctx], 2, (p, p, p, p)
    del tt
    n_keys = c if ctx_queries else s + c
    tk = min(DIFF_TK, n_keys)
    n_pad = -(-n_keys // tk) * tk
    return pl.pallas_call(
        functools.partial(_diff_kernel, n_src=n_src, tk=tk),
        grid=(b, DIFF_HEADS, nq),
        in_specs=[pl.BlockSpec(memory_space=pltpu.SMEM), q_spec] + kv_specs
                 + [pl.BlockSpec((1, LANES), lambda bi, h, qi: (0, 0))],
        out_specs=o_spec,
        out_shape=jax.ShapeDtypeStruct((out_rows, DIFF_HEADS * LANES), BF16),
        scratch_shapes=[pltpu.VMEM((n_pad, LANES), BF16), pltpu.VMEM((n_pad, 2 * LANES), BF16)]
                       + [pltpu.VMEM((tq, tk), F32), pltpu.VMEM((tq, tk), F32),
                          pltpu.VMEM((tq, tk), BF16), pltpu.VMEM((tq, tk), BF16),
                          pltpu.VMEM((tq, 1), F32), pltpu.VMEM((tq, 1), F32),
                          pltpu.VMEM((tq, 2 * LANES), F32), pltpu.VMEM((tq, 1), F32)] * 2,
        compiler_params=_cparams(("arbitrary", "arbitrary", "arbitrary")),
        name="diff_ctx" if ctx_queries else "diff_lat",
    )(scal, p, *kv_args, subln_g)


def _roll_bf16(a, shift):
    return pltpu.roll(a.astype(F32), shift, 1).astype(BF16)


def _gqa_core(sink_ref, q_ref, kk, vv, mask, o_ref):
    tq = q_ref.shape[0]
    lane = lax.broadcasted_iota(jnp.int32, (1, LANES), 1)
    half_of_lane = lax.shift_right_logical(lane, 6)
    kk_sw, vv_sw = _roll_bf16(kk, 64), _roll_bf16(vv, 64)
    row = lax.broadcasted_iota(jnp.int32, (2 * tq, 1), 0)
    for kvh in range(GQA_KV_HEADS):
        pairs = (2 * kvh, 2 * kvh + 1)
        outs = []
        for hh in range(2):
            kmat = kk if kvh == hh else kk_sw
            vmat = vv if kvh == hh else vv_sw
            qp = [q_ref[:, p * LANES:(p + 1) * LANES] for p in pairs]
            qs = jnp.concatenate([jnp.where(half_of_lane == hh, x, jnp.zeros_like(x)) for x in qp], axis=0)
            s = _dot_nt(qs, kmat)
            if mask is not None:
                s = jnp.where(mask, s, NEG_BIG)
            sink = jnp.where(row < tq, sink_ref[2 * pairs[0] + hh], sink_ref[2 * pairs[1] + hh])
            m = jnp.maximum(jnp.max(s, axis=-1, keepdims=True), sink)
            pr = jnp.exp(s - m)
            den = jnp.sum(pr, axis=-1, keepdims=True) + jnp.exp(sink - m)
            outs.append(_dot(pr.astype(BF16), vmat) / den)
        for pi, p in enumerate(pairs):
            rs = slice(pi * tq, (pi + 1) * tq)
            o_ref[:, p * LANES:(p + 1) * LANES] = jnp.where(half_of_lane == 0, outs[0][rs], outs[1][rs]).astype(BF16)


def _gqa_lat_kernel(sink_ref, q_ref, kp_ref, ko_ref, kn_ref, kc_ref, vp_ref, vo_ref, vn_ref, vc_ref, o_ref, *, seq):
    tq = q_ref.shape[0]
    qi = pl.program_id(1)
    kk = jnp.concatenate([kp_ref[...], ko_ref[...], kn_ref[...], kc_ref[...]], axis=0)
    vv = jnp.concatenate([vp_ref[...], vo_ref[...], vn_ref[...], vc_ref[...]], axis=0)
    nk = kk.shape[0]
    n_loc = tq + 2 * WINDOW
    r2 = lax.broadcasted_iota(jnp.int32, (2 * tq, nk), 0)
    r = jnp.where(r2 >= tq, r2 - tq, r2)
    c = lax.broadcasted_iota(jnp.int32, (2 * tq, nk), 1)
    kpos = qi * tq - WINDOW + c
    mask = ((jnp.abs(r - c + WINDOW) <= WINDOW) & (kpos >= 0) & (kpos < seq)) | (c >= n_loc)
    _gqa_core(sink_ref, q_ref, kk, vv, mask, o_ref)


def _gqa_ctx_kernel(sink_ref, q_ref, kc_ref, vc_ref, o_ref):
    _gqa_core(sink_ref, q_ref, kc_ref[...], vc_ref[...], None, o_ref)


def _gqa(p, sink, dims, ctx_queries):
    b, s, c = dims["batch"], dims["seq"], dims["ctx"]
    qw = GQA_HEADS * GQA_DH
    qc0, kc0, vc0 = COL_GQA_Q // qw, COL_GQA_K // LANES, COL_GQA_V // LANES
    ctx_blk = (b * s) // c
    sm = pl.BlockSpec(memory_space=pltpu.SMEM)
    if ctx_queries:
        return pl.pallas_call(
            _gqa_ctx_kernel,
            grid=(b,),
            in_specs=[sm, pl.BlockSpec((c, qw), lambda bi: (ctx_blk + bi, qc0)),
                      pl.BlockSpec((c, LANES), lambda bi: (ctx_blk + bi, kc0)),
                      pl.BlockSpec((c, LANES), lambda bi: (ctx_blk + bi, vc0))],
            out_specs=pl.BlockSpec((c, qw), lambda bi: (bi, 0)),
            out_shape=jax.ShapeDtypeStruct((b * c, qw), BF16),
            compiler_params=_cparams(("arbitrary",)),
            name="gqa_ctx",
        )(sink, p, p, p)
    tq = GQA_TQ
    nq = s // tq
    wpq = tq // WINDOW
    nwb = s // WINDOW

    def loc_specs(col):
        return [pl.BlockSpec((WINDOW, LANES), lambda bi, qi: (bi * nwb + jnp.maximum(qi * wpq - 1, 0), col)),
                pl.BlockSpec((tq, LANES), lambda bi, qi: (bi * nq + qi, col)),
                pl.BlockSpec((WINDOW, LANES), lambda bi, qi: (bi * nwb + jnp.minimum(qi * wpq + wpq, nwb - 1), col)),
                pl.BlockSpec((c, LANES), lambda bi, qi: (ctx_blk + bi, col))]

    return pl.pallas_call(
        functools.partial(_gqa_lat_kernel, seq=s),
        grid=(b, nq),
        in_specs=[sm, pl.BlockSpec((tq, qw), lambda bi, qi: (bi * nq + qi, qc0))] + loc_specs(kc0) + loc_specs(vc0),
        out_specs=pl.BlockSpec((tq, qw), lambda bi, qi: (bi * nq + qi, 0)),
        out_shape=jax.ShapeDtypeStruct((b * s, qw), BF16),
        compiler_params=_cparams(("arbitrary", "arbitrary")),
        name="gqa_lat",
    )(sink, p, *([p] * 8))


def _merge_kernel(x_ref, yr_ref, ydl_ref, ydc_ref, ygl_ref, ygc_ref, gmix_ref, gffn_ref, sh1_ref, sc1_ref, ga1_ref,
                  sh2_ref, sc2_ref, wg_ref, wb_ref, wo_ref, wrh_ref, wrl_ref, br_ref,
                  x1_ref, h2_ref, ti_ref, tw_ref, *, n_lat_tiles):
    d = x_ref.shape[1]
    x = x_ref[...]
    h = (_rms(x, gmix_ref[...]) * (1.0 + sc1_ref[0]) + sh1_ref[0]).astype(BF16)
    rw = RET_HEADS * RET_DV
    dw = DIFF_HEADS * LANES
    is_lat = pl.program_id(0) < n_lat_tiles
    yd = jnp.where(is_lat, ydl_ref[...], ydc_ref[...])
    yg = jnp.where(is_lat, ygl_ref[...], ygc_ref[...])
    merged = None
    for bi, (y, r0, r1) in enumerate(((yr_ref[...], 0, rw), (yd, rw, rw + dw), (yg, rw + dw, wb_ref.shape[0]))):
        gate = _sigmoid(_dot(h, wg_ref[:, bi * d:(bi + 1) * d]))
        term = gate * _dot(y, wb_ref[r0:r1, :])
        merged = term if merged is None else merged + term
    x1 = x + ga1_ref[0] * _dot(merged.astype(BF16), wo_ref[...])
    x1_ref[...] = x1
    h2 = _rms(x1, gffn_ref[...]) * (1.0 + sc2_ref[0]) + sh2_ref[0]
    tm, ns = x_ref.shape[0], d // LANES
    for s in range(ns):
        h2_ref[pl.ds(s, tm, stride=ns), :] = h2[:, s * LANES:(s + 1) * LANES]
    hh, hl = _split_bf16(h2)
    wrh, wrl = wrh_ref[...], wrl_ref[...]
    logits = _dot(hh, wrh) + (_dot(hl, wrh) + _dot(hh, wrl)) + br_ref[...]
    eidx = lax.broadcasted_iota(jnp.int32, logits.shape, 1).astype(F32)
    vals = []
    for k in range(TOP_K):
        mx = jnp.max(logits, axis=1, keepdims=True)
        ix = jnp.min(jnp.where(logits == mx, eidx, float(N_EXPERTS)), axis=1, keepdims=True)
        vals.append(mx)
        ti_ref[:, k:k + 1] = ix.astype(jnp.int32)
        logits = jnp.where(eidx == ix, -jnp.inf, logits)
    es = [jnp.exp(v - vals[0]) for v in vals]
    tot = es[0] + es[1] + es[2] + es[3]
    for k in range(TOP_K):
        tw_ref[:, k:k + 1] = es[k] / tot


def _merge(xall, yr, yd, yg, gmix, gffn, mod3, wg, wb, wo, wrh, wrl, br, dims, ntok):
    d = xall.shape[1]
    tm = dims["tm_merge"]
    tps = dims["seq"] // tm
    n_lat_tiles = dims["batch"] * tps

    def mrow(i):
        return jnp.minimum(i // tps, dims["batch"])

    def modspec(k):
        return pl.BlockSpec((1, 1, d), lambda i: (mrow(i), 0, k))

    def full(a):
        return pl.BlockSpec(a.shape, lambda i: (0,) * a.ndim)

    def rows(w):
        return pl.BlockSpec((tm, w), lambda i: (i, 0))

    def lat_ctx(pair):
        lat, ctx = pair
        w = lat.shape[1]
        lat_spec = pl.BlockSpec((tm, w), lambda i: (jnp.minimum(i, n_lat_tiles - 1), 0))
        if ctx is None:
            return [lat_spec, lat_spec], [lat, lat]
        n_ctx_tiles = ctx.shape[0] // tm
        ctx_spec = pl.BlockSpec((tm, w), lambda i: (jnp.clip(i - n_lat_tiles, 0, n_ctx_tiles - 1), 0))
        return [lat_spec, ctx_spec], [lat, ctx]

    yd_specs, yd_args = lat_ctx(yd)
    yg_specs, yg_args = lat_ctx(yg)
    return pl.pallas_call(
        functools.partial(_merge_kernel, n_lat_tiles=n_lat_tiles),
        grid=(ntok // tm,),
        in_specs=[rows(d), rows(yr.shape[1])] + yd_specs + yg_specs + [full(gmix), full(gffn),
                  modspec(0), modspec(1), modspec(2), modspec(3), modspec(4),
                  full(wg), full(wb), full(wo), full(wrh), full(wrl), full(br)],
        out_specs=[rows(d), pl.BlockSpec((tm * (d // LANES), LANES), lambda i: (i, 0)),
                   pl.BlockSpec((tm, TOP_K), lambda i: (i, 0)), pl.BlockSpec((tm, TOP_K), lambda i: (i, 0))],
        out_shape=[jax.ShapeDtypeStruct((ntok, d), F32), jax.ShapeDtypeStruct((ntok * (d // LANES), LANES), F32),
                   jax.ShapeDtypeStruct((ntok, TOP_K), jnp.int32), jax.ShapeDtypeStruct((ntok, TOP_K), F32)],
        compiler_params=_cparams(("arbitrary",)),
        name="merge",
    )(xall, yr, *yd_args, *yg_args, gmix, gffn, mod3, mod3, mod3, mod3, mod3, wg, wb, wo, wrh, wrl, br)


TOK_BITS = 15
FFN_CHUNK = 256
GATHER_DEPTH = 3


def _moe_kernel(be_ref, nv_ref, nu_ref, base_ref, code_ref, h2_hbm, wgu_ref, bgu_ref, wdn_ref, bdn_ref, o_hbm,
                xbuf, ybuf, x_scr, y_scr, wgu_bf, wdn_bf, gsem, ssem):
    i = pl.program_id(0)
    nu = nu_ref[0]
    slot = i % 2
    gslot = lax.rem(i, GATHER_DEPTH)
    last_blk = pl.num_programs(0) - 1
    nb = MOE_BLOCK
    d = wdn_bf.shape[0]
    ns = d // LANES

    def gather_rows(blk, sl, r0, r1):
        base = base_ref[blk]
        for r in range(r0, r1):
            tok = code_ref[base + r] & ((1 << TOK_BITS) - 1)
            pltpu.make_async_copy(h2_hbm.at[pl.ds(pl.multiple_of(tok * ns, ns), ns)], xbuf.at[sl, pl.ds(r * ns, ns)],
                                  gsem.at[sl]).start(priority=r % 2)

    def wait_rows(n, copy_of_rows):
        @pl.when(n == nb)
        def _():
            copy_of_rows(nb).wait()

        @pl.when(n != nb)
        def _():
            def body(r, carry):
                copy_of_rows(1).wait()
                return carry
            lax.fori_loop(0, n, body, 0)

    def gather_wait(sl):
        pltpu.make_async_copy(h2_hbm.at[pl.ds(0, nb * ns)], xbuf.at[sl], gsem.at[sl]).wait()

    def scatter_rows(blk, sl, r0, r1):
        base = base_ref[blk]
        n = nv_ref[blk]

        def body(r):
            dst = lax.shift_right_logical(code_ref[base + r], TOK_BITS)
            pltpu.make_async_copy(ybuf.at[sl, pl.ds(pl.multiple_of(r * ns, ns), ns)],
                                  o_hbm.at[pl.ds(pl.multiple_of(dst * ns, ns), ns)], ssem.at[sl]
                                  ).start(priority=r % 2 if isinstance(r, int) else 0)

        @pl.when(n == nb)
        def _():
            for r in range(r0, r1):
                body(r)

        @pl.when(n != nb)
        def _():
            def step(r, carry):
                body(r)
                return carry
            lax.fori_loop(r0, jnp.clip(n, r0, r1), step, 0)

    def scatter_wait(blk, sl):
        wait_rows(nv_ref[blk], lambda n: pltpu.make_async_copy(
            ybuf.at[sl, pl.ds(0, n * ns)], o_hbm.at[pl.ds(0, n * ns)], ssem.at[sl]))

    @pl.when(i < nu)
    def _():
        @pl.when(i == 0)
        def _():
            gather_rows(0, 0, 0, nb)
            gather_rows(jnp.minimum(1, last_blk), 1, 0, nb)

        @pl.when((i == 0) | (be_ref[i] != be_ref[jnp.maximum(i - 1, 0)]))
        def _():
            rc = 128

            def cast(r, carry):
                r0 = pl.multiple_of(r * rc, rc)
                wgu_bf[pl.ds(r0, rc), :] = wgu_ref[0, 0, pl.ds(r0, rc), :].astype(BF16)
                wdn_bf[pl.ds(r0, rc), :] = wdn_ref[0, 0, pl.ds(r0, rc), :].astype(BF16)
                return carry
            lax.fori_loop(0, d // rc, cast, 0)

        gather_wait(gslot)
        nxt = jnp.minimum(i + GATHER_DEPTH - 1, last_blk)
        nslot = lax.rem(i + GATHER_DEPTH - 1, GATHER_DEPTH)
        x_scr[...] = jnp.concatenate([xbuf[gslot, pl.ds(s, nb, stride=ns), :] for s in range(ns)],
                                     axis=1).astype(BF16)
        cw = FFN_CHUNK
        nch = d // cw
        rpc = nb // nch
        for ci in range(nch):
            c0, c1 = ci * cw, (ci + 1) * cw
            x = x_scr[...]
            glu = jnp.minimum(_dot(x, wgu_bf[:, c0:c1]) + bgu_ref[0, 0, :, c0:c1], SWIGLU_LIMIT)
            lin = jnp.clip(_dot(x, wgu_bf[:, d + c0:d + c1]) + bgu_ref[0, 0, :, d + c0:d + c1],
                           -SWIGLU_LIMIT, SWIGLU_LIMIT)
            act = glu * _sigmoid(SWIGLU_ALPHA * glu) * (lin + 1.0)
            part = _dot(act.astype(BF16), wdn_bf[c0:c1, :])
            y_scr[...] = part + bdn_ref[0, 0] if ci == 0 else y_scr[...] + part

            @pl.when(i >= 0)
            def _(ci=ci):
                gather_rows(nxt, nslot, ci * rpc, (ci + 1) * rpc)

                @pl.when(i >= 1)
                def _():
                    scatter_rows(i - 1, 1 - slot, ci * rpc, (ci + 1) * rpc)

        @pl.when(i >= 2)
        def _():
            scatter_wait(i - 2, slot)

        for s in range(ns):
            ybuf[slot, pl.ds(s, nb, stride=ns), :] = y_scr[:, s * LANES:(s + 1) * LANES]

        @pl.when(i == nu - 1)
        def _():
            scatter_rows(i, slot, 0, nb)
            gather_wait(lax.rem(i + 1, GATHER_DEPTH))
            gather_wait(nslot)

            @pl.when(i >= 1)
            def _():
                scatter_wait(i - 1, 1 - slot)
            scatter_wait(i, slot)


def _moe(h2, route, w_gu, b_gu, w_dn, b_dn, layer):
    codes, block_expert, n_valid, n_used, base = route
    d = w_dn.shape[-1]
    ns = d // LANES
    ntok = h2.shape[0] // ns
    n_blocks = block_expert.shape[0]
    depth, ne = w_gu.shape[:2]

    def wspec(rows, cols):
        return pl.BlockSpec((1, 1, rows, cols), lambda i, be, nv, nu, bs, cd: (layer, be[i], 0, 0))

    grid_spec = pltpu.PrefetchScalarGridSpec(
        num_scalar_prefetch=5,
        grid=(n_blocks,),
        in_specs=[pl.BlockSpec(memory_space=pl.ANY),
                  wspec(d, 2 * d), wspec(1, 2 * d), wspec(d, d), wspec(1, d)],
        out_specs=pl.BlockSpec(memory_space=pl.ANY),
        scratch_shapes=[pltpu.VMEM((GATHER_DEPTH, MOE_BLOCK * ns, LANES), F32),
                        pltpu.VMEM((2, MOE_BLOCK * ns, LANES), F32),
                        pltpu.VMEM((MOE_BLOCK, d), BF16), pltpu.VMEM((MOE_BLOCK, d), F32),
                        pltpu.VMEM((d, 2 * d), BF16), pltpu.VMEM((d, d), BF16),
                        pltpu.SemaphoreType.DMA((GATHER_DEPTH,)), pltpu.SemaphoreType.DMA((2,))])
    return pl.pallas_call(
        _moe_kernel,
        grid_spec=grid_spec,
        out_shape=jax.ShapeDtypeStruct((TOP_K * ntok * ns, LANES), F32),
        compiler_params=_cparams(("arbitrary",)),
        name="moe",
    )(block_expert, n_valid, n_used, base, codes, h2, w_gu, b_gu.reshape(depth, ne, 1, 2 * d), w_dn,
      b_dn.reshape(depth, ne, 1, d))


def _route(ti, ntok):
    a = TOP_K * ntok
    n_blocks = (a + N_EXPERTS * (MOE_BLOCK - 1) + MOE_BLOCK - 1) // MOE_BLOCK
    expert = ti.T.reshape(a)
    aid_bits = (a - 1).bit_length()
    assert aid_bits + 5 < 32 and N_EXPERTS <= 32
    aid = jnp.arange(a, dtype=jnp.int32)
    aid_sorted = jnp.sort(lax.shift_left(expert, aid_bits) | aid) & ((1 << aid_bits) - 1)
    codes = lax.shift_left(aid_sorted, TOK_BITS) | (aid_sorted % ntok)
    codes = jnp.concatenate([codes, jnp.zeros((MOE_BLOCK,), jnp.int32)])
    counts = jnp.sum((expert[:, None] == jnp.arange(N_EXPERTS, dtype=jnp.int32)[None, :]).astype(jnp.int32), axis=0)
    padded = (counts + MOE_BLOCK - 1) // MOE_BLOCK * MOE_BLOCK
    pad_end = jnp.cumsum(padded)
    pad_start = pad_end - padded
    cstart = jnp.cumsum(counts) - counts
    n_used = (pad_end[-1:] // MOE_BLOCK).astype(jnp.int32)
    block_start = jnp.arange(n_blocks, dtype=jnp.int32) * MOE_BLOCK
    block_expert = jnp.minimum(jnp.sum((pad_end[None, :] <= block_start[:, None]).astype(jnp.int32), axis=1),
                               N_EXPERTS - 1)
    n_valid = jnp.clip(counts[block_expert] - (block_start - pad_start[block_expert]), 0, MOE_BLOCK).astype(jnp.int32)
    base = jnp.clip(block_start - (pad_start - cstart)[block_expert], 0, a).astype(jnp.int32)
    return codes, block_expert, n_valid, n_used, base


def _combine_kernel(x1_ref, o0_ref, o1_ref, o2_ref, o3_ref, tw_ref, ga2_ref, gf_ref, x2_ref, *, final):
    o_refs = (o0_ref, o1_ref, o2_ref, o3_ref)
    tm, ns = x1_ref.shape[0], x1_ref.shape[1] // LANES
    w = [jnp.broadcast_to(tw_ref[:, k:k + 1], (tm, LANES)) for k in range(TOP_K)]
    for s in range(ns):
        cols = slice(s * LANES, (s + 1) * LANES)
        y = None
        for k in range(TOP_K):
            term = w[k] * o_refs[k][pl.ds(s, tm, stride=ns), :]
            y = term if y is None else y + term
        x2_ref[:, cols] = x1_ref[:, cols] + ga2_ref[0, :, cols] * y
    if final:
        x2_ref[...] = _rms(x2_ref[...], gf_ref[...])


def _combine(x1, o4, tw, mod3, g_final, dims, final):
    ntok, d = x1.shape
    tm = dims["tm_merge"]
    tps = dims["seq"] // tm
    nt = ntok // tm

    def ospec(k):
        return pl.BlockSpec((tm * (d // LANES), LANES), lambda i: (k * nt + i, 0))

    return pl.pallas_call(
        functools.partial(_combine_kernel, final=final),
        grid=(nt,),
        in_specs=[pl.BlockSpec((tm, d), lambda i: (i, 0)), ospec(0), ospec(1), ospec(2), ospec(3),
                  pl.BlockSpec((tm, TOP_K), lambda i: (i, 0)),
                  pl.BlockSpec((1, 1, d), lambda i: (jnp.minimum(i // tps, dims["batch"]), 0, 5)),
                  pl.BlockSpec((1, d), lambda i: (0, 0))],
        out_specs=pl.BlockSpec((tm, d), lambda i: (i, 0)),
        out_shape=jax.ShapeDtypeStruct((ntok, d), F32),
        compiler_params=_cparams(("arbitrary",)),
        name="combine",
    )(x1, o4, o4, o4, o4, tw, mod3, g_final)


def _rope_tables(seq, pad_rows):
    pos = jnp.arange(seq, dtype=jnp.int32)
    row = (pos // GRID_W).astype(F32)
    col = (pos % GRID_W).astype(F32)

    def cs(head_dim):
        quarter = head_dim // 4
        inv_freq = ROPE_BASE ** (-jnp.arange(quarter, dtype=F32) / quarter)
        ang = jnp.concatenate([row[:, None] * inv_freq, col[:, None] * inv_freq], axis=-1)
        return jnp.cos(ang), jnp.sin(ang)

    c128, s128 = cs(RET_DK)
    c64, s64 = cs(DIFF_DK)
    ones = jnp.ones((pad_rows, LANES), F32)
    zeros = jnp.zeros((pad_rows, LANES), F32)
    a128 = jnp.concatenate([jnp.concatenate([c128, c128], -1), ones], 0)
    b128 = jnp.concatenate([jnp.concatenate([-s128, s128], -1), zeros], 0)
    a64 = jnp.concatenate([jnp.concatenate([c64, c64, c64, c64], -1), ones], 0)
    b64 = jnp.concatenate([jnp.concatenate([-s64, s64, -s64, s64], -1), zeros], 0)
    return a128, b128, a64, b64


def _pick_tile(cands, *sizes):
    for t in cands:
        if all(s % t == 0 for s in sizes):
            return t
    raise ValueError("no tile size fits")


def kernel(x, c, ctx, c_ctx, w_mod, b_mod, g_norm_mix, g_norm_ffn, w_in, ret_decay_fwd, ret_decay_bwd, diff_lambda_q1, diff_lambda_k1, diff_lambda_q2, diff_lambda_k2, diff_subln_g, gqa_sink, w_branch, w_out, w_router, b_router, w_gate_up, b_gate_up, w_down, b_down, g_final):
    b, s, d = x.shape
    n_ctx = ctx.shape[1]
    depth = w_mod.shape[0]
    n_lat = b * s
    tt = n_lat + b * n_ctx
    assert s % RET_CHUNK == 0 and n_ctx % RET_CHUNK == 0 and s % GQA_TQ == 0 and n_lat % n_ctx == 0
    assert tt < (1 << TOK_BITS) and b < 8
    tm = _pick_tile((1024, 512, 256), s, b * n_ctx)
    tm_merge = _pick_tile((512, 256), s, b * n_ctx)
    dims = dict(batch=b, seq=s, ctx=n_ctx, tm=tm, tm_merge=tm_merge,
                n_lat_tiles=n_lat // tm, tiles_per_seq=s // tm)

    xall = jnp.concatenate([x.reshape(n_lat, d), ctx.reshape(b * n_ctx, d)], axis=0)
    cvec = jnp.concatenate([c, c_ctx[None, :], jnp.zeros((8 - b - 1, d), F32)], axis=0)
    mod = _mod_all(cvec, w_mod, b_mod)
    tabs = _rope_tables(s, tm)

    for l in range(depth):
        last = l == depth - 1
        mod3 = mod[l].reshape(8, 1, 6 * d)
        w_l = w_in[l]
        p = _inproj(xall, g_norm_mix[l][None, :], mod3, w_in, l, tabs, dims)

        lg = jnp.stack([jax.nn.log_sigmoid(ret_decay_fwd[l].astype(F32)),
                        jax.nn.log_sigmoid(ret_decay_bwd[l].astype(F32))], axis=0)
        y_r = _retention(p, lg, dims)

        lam_init = 0.8 - 0.6 * math.exp(-0.3 * l)
        lam = (jnp.exp(jnp.sum(diff_lambda_q1[l].astype(F32) * diff_lambda_k1[l].astype(F32)))
               - jnp.exp(jnp.sum(diff_lambda_q2[l].astype(F32) * diff_lambda_k2[l].astype(F32))) + lam_init)
        scal = jnp.stack([lam, jnp.asarray(1.0 - lam_init, F32)])
        subln = diff_subln_g[l][None, :].astype(F32)
        sink = gqa_sink[l].astype(F32)
        y_d = (_diff_attention(p, scal, subln, dims, False),
               None if last else _diff_attention(p, scal, subln, dims, True))
        y_g = (_gqa(p, sink, dims, False), None if last else _gqa(p, sink, dims, True))
        ntok = n_lat if last else tt

        wrh, wrl = _split_bf16(w_router[l])
        x1, h2, ti, tw = _merge(xall, y_r, y_d, y_g, g_norm_mix[l][None, :], g_norm_ffn[l][None, :], mod3,
                                w_l[:, P_COLS:].astype(BF16), w_branch[l].astype(BF16), w_out[l].astype(BF16),
                                wrh, wrl, b_router[l][None, :].astype(F32), dims, ntok)
        o4 = _moe(h2, _route(ti, ntok), w_gate_up, b_gate_up, w_down, b_down, l)
        xall = _combine(x1, o4, tw, mod3, g_final[None, :], dims, last)
    return xall.reshape(b, s, d)
```

```python
import functools
import math

import jax
import jax.numpy as jnp
from jax import lax
from jax.experimental import pallas as pl
from jax.experimental.pallas import tpu as pltpu

F32 = jnp.float32
BF16 = jnp.bfloat16

GRID_W = 64
RET_HEADS, RET_DK, RET_DV = 4, 128, 256
DIFF_HEADS, DIFF_DK = 4, 64
GQA_HEADS, GQA_KV_HEADS, GQA_DH = 8, 2, 64
WINDOW = 128
N_EXPERTS, TOP_K = 32, 4
SWIGLU_ALPHA, SWIGLU_LIMIT = 1.702, 7.0
MOE_BLOCK = 256
ROPE_BASE = 10000.0
NORM_EPS = 1e-6
NEG_BIG = -1e30

LANES = 128
P_COLS = 5376
IN_TN = 768
RET_CHUNK = 256
DIFF_TQ, DIFF_TK = 512, 512
DIFF_ROW_CHUNK = 64
LOG2E = math.log2(math.e)
GQA_TQ = 256
VMEM_LIMIT = 56 * 1024 * 1024

_PLAIN = ("plain", 1.0)
_GROUP_KINDS = ([("r128", RET_DK ** -0.5)] * 4 + [("r128", 1.0)] * 4 + [_PLAIN] * 16
                + [("r64", DIFF_DK ** -0.5 * LOG2E)] * 4 + [("r64", 1.0)] * 4 + [_PLAIN] * 4
                + [("r64", GQA_DH ** -0.5)] * 4 + [("r64", 1.0)] + [_PLAIN])
COL_RET_Q, COL_RET_K, COL_RET_V, COL_RET_G = 0, 512, 1024, 2048
COL_DIFF_Q, COL_DIFF_K, COL_DIFF_V = 3072, 3584, 4096
COL_GQA_Q, COL_GQA_K, COL_GQA_V = 4608, 5120, 5248


def _cparams(sem):
    return pltpu.CompilerParams(dimension_semantics=sem, vmem_limit_bytes=VMEM_LIMIT)


def _sigmoid(x):
    return 1.0 / (1.0 + jnp.exp(-x))


def _dot(a, b):
    return jnp.dot(a, b, preferred_element_type=F32)


def _dot_nt(a, b):
    return lax.dot_general(a, b, (((1,), (1,)), ((), ())), preferred_element_type=F32)


def _dot_tn(a, b):
    return lax.dot_general(a, b, (((0,), (0,)), ((), ())), preferred_element_type=F32)


def _split_bf16(a):
    hi = a.astype(BF16)
    lo = (a - hi.astype(F32)).astype(BF16)
    return hi, lo


def _rms(x, g):
    return x * lax.rsqrt(jnp.mean(x * x, axis=-1, keepdims=True) + NORM_EPS) * g


def _mod_kernel(c_ref, w_ref, b_ref, o_ref):
    c = c_ref[...]
    s = c * _sigmoid(c)
    sh, sl = _split_bf16(s)
    wh, wl = _split_bf16(w_ref[0])
    o_ref[0] = _dot(sh, wh) + (_dot(sl, wh) + _dot(sh, wl)) + b_ref[0]


def _mod_all(cvec, w_mod, b_mod):
    depth, d, n = w_mod.shape
    tn = n // 4
    return pl.pallas_call(
        _mod_kernel,
        grid=(depth, n // tn),
        in_specs=[pl.BlockSpec((8, d), lambda l, j: (0, 0)),
                  pl.BlockSpec((1, d, tn), lambda l, j: (l, 0, j)),
                  pl.BlockSpec((1, 1, tn), lambda l, j: (l, 0, j))],
        out_specs=pl.BlockSpec((1, 8, tn), lambda l, j: (l, 0, j)),
        out_shape=jax.ShapeDtypeStruct((depth, 8, n), F32),
        compiler_params=_cparams(("arbitrary", "arbitrary")),
        name="mod",
    )(cvec, w_mod, b_mod.reshape(depth, 1, n))


def _rope_group(a, kind, scale, a128, b128, a64, b64):
    if kind == "plain":
        return a
    if kind == "r128":
        out = a * a128 + pltpu.roll(a, 64, 1) * b128
    else:
        lane = lax.broadcasted_iota(jnp.int32, (1, LANES), 1)
        first = (lane & 63) < 32
        partner = jnp.where(first, pltpu.roll(a, 96, 1), pltpu.roll(a, 32, 1))
        out = a * a64 + partner * b64
    return out * scale if scale != 1.0 else out


def _inproj_kernel(x_ref, g_ref, sc_ref, sh_ref, w_ref, a128_ref, b128_ref, a64_ref, b64_ref,
                   o_ref, h_scr, acc0_scr, acc1_scr, *, n_tiles):
    j = pl.program_id(1)
    accs = (acc0_scr, acc1_scr)

    @pl.when(j == 0)
    def _():
        y = _rms(x_ref[...], g_ref[...])
        h_scr[...] = (y * (1.0 + sc_ref[0]) + sh_ref[0]).astype(BF16)

    gpt = IN_TN // LANES
    for jj in range(n_tiles + 1):
        @pl.when(j == jj)
        def _(jj=jj):
            if jj < n_tiles:
                accs[jj % 2][...] = _dot(h_scr[...], w_ref[0].astype(BF16))
            if jj >= 1:
                acc = accs[(jj - 1) % 2]
                for g in range(gpt):
                    kind, scale = _GROUP_KINDS[(jj - 1) * gpt + g]
                    a = acc[:, g * LANES:(g + 1) * LANES]
                    out = _rope_group(a, kind, scale, a128_ref[...], b128_ref[...], a64_ref[...], b64_ref[...])
                    o_ref[:, g * LANES:(g + 1) * LANES] = out.astype(BF16)


def _inproj(xall, g, mod3, w_in, layer, tabs, dims):
    tt, d = xall.shape
    tm = dims["tm"]
    n_lat_tiles, tps = dims["n_lat_tiles"], dims["tiles_per_seq"]
    n_tiles = P_COLS // IN_TN

    def mrow(i):
        return jnp.minimum(i // tps, dims["batch"])

    def trow(i):
        return jnp.where(i < n_lat_tiles, i % tps, tps)

    tab_spec = pl.BlockSpec((tm, LANES), lambda i, j: (trow(i), 0))
    return pl.pallas_call(
        functools.partial(_inproj_kernel, n_tiles=n_tiles),
        grid=(tt // tm, n_tiles + 1),
        in_specs=[pl.BlockSpec((tm, d), lambda i, j: (i, 0)),
                  pl.BlockSpec((1, d), lambda i, j: (0, 0)),
                  pl.BlockSpec((1, 1, d), lambda i, j: (mrow(i), 0, 1)),
                  pl.BlockSpec((1, 1, d), lambda i, j: (mrow(i), 0, 0)),
                  pl.BlockSpec((1, d, IN_TN), lambda i, j: (layer, 0, jnp.minimum(j, n_tiles - 1))),
                  tab_spec, tab_spec, tab_spec, tab_spec],
        out_specs=pl.BlockSpec((tm, IN_TN), lambda i, j: (i, jnp.maximum(j - 1, 0))),
        out_shape=jax.ShapeDtypeStruct((tt, P_COLS), BF16),
        scratch_shapes=[pltpu.VMEM((tm, d), BF16), pltpu.VMEM((tm, IN_TN), F32), pltpu.VMEM((tm, IN_TN), F32)],
        compiler_params=_cparams(("arbitrary", "arbitrary")),
        name="inproj",
    )(xall, g, mod3, mod3, w_in, *tabs)


def _ret_kernel(lg_ref, q_ref, k_ref, v_ref, g_ref, y_ref, s_scr, of_scr, o_scr, *, ncc, ncl):
    ph = pl.program_id(1)
    t = pl.program_id(2)
    ch = RET_CHUNK

    @pl.when(t == 0)
    def _():
        s_scr[...] = jnp.zeros_like(s_scr)

    fwd = ph == 0
    ii = lax.broadcasted_iota(jnp.int32, (ch, ch), 0).astype(F32)
    jj = lax.broadcasted_iota(jnp.int32, (ch, ch), 1).astype(F32)
    rel = jnp.where(fwd, ii - jj, jj - ii)
    pos = lax.broadcasted_iota(jnp.int32, (ch, 1), 0).astype(F32)
    qpos = jnp.where(fwd, pos + 1.0, ch - pos)
    kpos = jnp.where(fwd, ch - 1.0 - pos, pos)
    mirror = jnp.where(t < ncc, ncc - 1 - t, ncc + ncl - 1 - (t - ncc))
    row0 = pl.multiple_of(jnp.where(fwd, t, mirror) * ch, ch)

    for h in range(RET_HEADS):
        lg = lg_ref[ph, h]
        dmask = jnp.where(rel >= 0.0, jnp.exp(jnp.maximum(rel, 0.0) * lg), 0.0)
        q = q_ref[:, h * RET_DK:(h + 1) * RET_DK]
        k = k_ref[:, h * RET_DK:(h + 1) * RET_DK]
        v = v_ref[:, h * RET_DV:(h + 1) * RET_DV]
        a = (_dot_nt(q, k) * dmask).astype(BF16)
        qd = (q.astype(F32) * jnp.exp(qpos * lg)).astype(BF16)
        kd = (k.astype(F32) * jnp.exp(kpos * lg)).astype(BF16)
        state = s_scr[h]
        o_scr[:, h * RET_DV:(h + 1) * RET_DV] = _dot(a, v) + _dot(qd, state.astype(BF16))
        s_scr[h] = state * jnp.exp(ch * lg) + _dot_tn(kd, v)

    @pl.when(fwd)
    def _():
        of_scr[pl.ds(row0, ch), :] = o_scr[...]

    @pl.when(jnp.logical_not(fwd))
    def _():
        for h in range(RET_HEADS):
            cs = slice(h * RET_DV, (h + 1) * RET_DV)
            ot = o_scr[:, cs] + of_scr[pl.ds(row0, ch), cs]
            mu = jnp.mean(ot, axis=-1, keepdims=True)
            oc = ot - mu
            var = jnp.mean(oc * oc, axis=-1, keepdims=True)
            on = oc * lax.rsqrt(var + NORM_EPS)
            gate = g_ref[:, cs].astype(F32)
            y_ref[:, cs] = (gate * _sigmoid(gate) * on).astype(BF16)


def _retention(p, lg, dims):
    tt = p.shape[0]
    ch = RET_CHUNK
    b, s, c = dims["batch"], dims["seq"], dims["ctx"]
    ncc, ncl = c // ch, s // ch
    nsteps = ncc + ncl
    ctx_base = (b * s) // ch

    def rblk(bi, ph, t):
        fwd_blk = jnp.where(t < ncc, ctx_base + bi * ncc + t, bi * ncl + (t - ncc))
        bwd_blk = jnp.where(t < ncc, ctx_base + bi * ncc + (ncc - 1 - t), bi * ncl + (ncl - 1 - (t - ncc)))
        return jnp.where(ph == 0, fwd_blk, bwd_blk)

    def oblk(bi, ph, t):
        return rblk(bi, 1, jnp.where(ph == 0, 0, t))

    qk_w = RET_HEADS * RET_DK
    v_w = RET_HEADS * RET_DV
    return pl.pallas_call(
        functools.partial(_ret_kernel, ncc=ncc, ncl=ncl),
        grid=(b, 2, nsteps),
        in_specs=[pl.BlockSpec(memory_space=pltpu.SMEM),
                  pl.BlockSpec((ch, qk_w), lambda bi, ph, t: (rblk(bi, ph, t), COL_RET_Q // qk_w)),
                  pl.BlockSpec((ch, qk_w), lambda bi, ph, t: (rblk(bi, ph, t), COL_RET_K // qk_w)),
                  pl.BlockSpec((ch, v_w), lambda bi, ph, t: (rblk(bi, ph, t), COL_RET_V // v_w)),
                  pl.BlockSpec((ch, v_w), lambda bi, ph, t: (rblk(bi, ph, t), COL_RET_G // v_w))],
        out_specs=pl.BlockSpec((ch, v_w), lambda bi, ph, t: (oblk(bi, ph, t), 0)),
        out_shape=jax.ShapeDtypeStruct((tt, v_w), BF16),
        scratch_shapes=[pltpu.VMEM((RET_HEADS, RET_DK, RET_DV), F32),
                        pltpu.VMEM((nsteps * ch, v_w), F32), pltpu.VMEM((ch, v_w), F32)],
        compiler_params=_cparams(("arbitrary", "arbitrary", "arbitrary")),
        name="retention",
    )(lg, p, p, p, p)


def _diff_kernel(sc_ref, q_ref, *refs, n_src, tk):
    kv = refs[:2 * n_src]
    g_ref, o_ref = refs[2 * n_src], refs[2 * n_src + 1]
    kx_scr, vx_scr = refs[2 * n_src + 2], refs[2 * n_src + 3]
    nbuf = 8
    streams = (refs[2 * n_src + 4:2 * n_src + 4 + nbuf], refs[2 * n_src + 4 + nbuf:2 * n_src + 4 + 2 * nbuf])
    tq = q_ref.shape[0]
    rc = DIFF_ROW_CHUNK
    n_keys = sum(kv[2 * si].shape[0] for si in range(n_src))
    n_tiles = kx_scr.shape[0] // tk

    @pl.when(pl.program_id(2) == 0)
    def _():
        vx_scr[:, LANES:] = jnp.ones((vx_scr.shape[0], LANES), BF16)
        r0 = 0
        for si in range(n_src):
            k_ref, v_ref = kv[2 * si], kv[2 * si + 1]
            kx_scr[r0:r0 + k_ref.shape[0], :] = k_ref[...]
            vx_scr[r0:r0 + v_ref.shape[0], :LANES] = v_ref[...]
            r0 += k_ref.shape[0]
        if r0 < kx_scr.shape[0]:
            kx_scr[r0:, :] = jnp.zeros((kx_scr.shape[0] - r0, LANES), BF16)
            vx_scr[r0:, :LANES] = jnp.zeros((kx_scr.shape[0] - r0, LANES), BF16)

    q = q_ref[...]
    lane = lax.broadcasted_iota(jnp.int32, (1, LANES), 1)
    zero = jnp.zeros_like(q)
    qs = (jnp.where(lane < DIFF_DK, q, zero), jnp.where(lane >= DIFF_DK, q, zero))
    S_BUF, P_BUF, AL_BUF, ACC, MAX = 0, 2, 4, 6, 7
    for bufs in streams:
        bufs[ACC][...] = jnp.zeros_like(bufs[ACC])
        bufs[MAX][...] = jnp.full(bufs[MAX].shape, NEG_BIG, F32)

    def tile_rows(t):
        return pl.ds(t * tk if isinstance(t, int) else pl.multiple_of(t * tk, tk), tk)

    def scores(t, par):
        kt = kx_scr[tile_rows(t), :]
        for st, bufs in enumerate(streams):
            bufs[S_BUF + par][...] = _dot_nt(qs[st], kt)

    def accumulate(t, par):
        vt = vx_scr[tile_rows(t), :]
        for bufs in streams:
            bufs[ACC][...] = bufs[AL_BUF + par][...] * bufs[ACC][...] + _dot(bufs[P_BUF + par][...], vt)

    def stage(t, par, prefetch, n_valid, drain_prev):
        if prefetch:
            scores(t + 1, 1 - par)
        if drain_prev:
            accumulate(t - 1, 1 - par)
        for bufs in streams:
            s_scr, p_scr, al_scr, m_scr = bufs[S_BUF + par], bufs[P_BUF + par], bufs[AL_BUF + par], bufs[MAX]
            m_all = m_scr[...]
            m_new, alpha = [], []
            for r in range(tq // rc):
                rows = slice(r * rc, (r + 1) * rc)
                s = s_scr[rows, :]
                if n_valid < tk:
                    col = lax.broadcasted_iota(jnp.int32, s.shape, 1)
                    s = jnp.where(col < n_valid, s, NEG_BIG)
                m_old = m_all[rows]
                mn = jnp.maximum(m_old, jnp.max(s, axis=-1, keepdims=True))
                p_scr[rows, :] = jnp.exp2(s - mn).astype(BF16)
                alpha.append(jnp.exp2(m_old - mn))
                m_new.append(mn)
            m_scr[...] = jnp.concatenate(m_new, axis=0)
            al_scr[...] = jnp.concatenate(alpha, axis=0)

    def n_valid_of(t):
        return n_keys - t * tk if t == n_tiles - 1 else tk

    scores(0, 0)
    stage(0, 0, n_tiles > 1, n_valid_of(0), False)
    pairs = max(0, (n_tiles - 2) // 2)
    if pairs > 0:
        def body(u, carry):
            stage(2 * u + 1, 1, True, tk, True)
            stage(2 * u + 2, 0, True, tk, True)
            return carry
        lax.fori_loop(0, pairs, body, 0)
    for t in range(2 * pairs + 1, n_tiles):
        stage(t, t % 2, t < n_tiles - 1, n_valid_of(t), True)
    accumulate(n_tiles - 1, (n_tiles - 1) % 2)
    lam = sc_ref[0]
    a1, a2 = streams[0][ACC], streams[1][ACC]
    o = a1[:, :LANES] / a1[:, LANES:] - lam * (a2[:, :LANES] / a2[:, LANES:])
    o = o * lax.rsqrt(jnp.mean(o * o, axis=-1, keepdims=True) + NORM_EPS) * g_ref[...] * sc_ref[1]
    o_ref[...] = o.astype(BF16)


def _diff_attention(p, scal, subln_g, dims, ctx_queries):
    b, s, c = dims["batch"], dims["seq"], dims["ctx"]
    tt = p.shape[0]
    qc0, kc0, vc0 = COL_DIFF_Q // LANES, COL_DIFF_K // LANES, COL_DIFF_V // LANES
    ctx_blk = (b * s) // c
    kctx = pl.BlockSpec((c, LANES), lambda bi, h, qi: (ctx_blk + bi, kc0 + h))
    vctx = pl.BlockSpec((c, LANES), lambda bi, h, qi: (ctx_blk + bi, vc0 + h))
    if ctx_queries:
        tq, nq = c, 1
        q_spec = pl.BlockSpec((tq, LANES), lambda bi, h, qi: (ctx_blk + bi, qc0 + h))
        o_spec = pl.BlockSpec((tq, LANES), lambda bi, h, qi: (bi, h))
        out_rows = b * c
        kv_specs, n_src, kv_args = [kctx, vctx], 1, (p, p)
    else:
        tq = min(DIFF_TQ, s)
        nq = s // tq
        q_spec = pl.BlockSpec((tq, LANES), lambda bi, h, qi: (bi * nq + qi, qc0 + h))
        o_spec = pl.BlockSpec((tq, LANES), lambda bi, h, qi: (bi * nq + qi, h))
        out_rows = b * s
        klat = pl.BlockSpec((s, LANES), lambda bi, h, qi: (bi, kc0 + h))
        vlat = pl.BlockSpec((s, LANES), lambda bi, h, qi: (bi, vc0 + h))
        kv_specs, n_src, kv_args = [klat, vlat, kctx, vctx], 2, (p, p, p, p)
    del tt
    n_keys = c if ctx_queries else s + c
    tk = min(DIFF_TK, n_keys)
    n_pad = -(-n_keys // tk) * tk
    return pl.pallas_call(
        functools.partial(_diff_kernel, n_src=n_src, tk=tk),
        grid=(b, DIFF_HEADS, nq),
        in_specs=[pl.BlockSpec(memory_space=pltpu.SMEM), q_spec] + kv_specs
                 + [pl.BlockSpec((1, LANES), lambda bi, h, qi: (0, 0))],
        out_specs=o_spec,
        out_shape=jax.ShapeDtypeStruct((out_rows, DIFF_HEADS * LANES), BF16),
        scratch_shapes=[pltpu.VMEM((n_pad, LANES), BF16), pltpu.VMEM((n_pad, 2 * LANES), BF16)]
                       + [pltpu.VMEM((tq, tk), F32), pltpu.VMEM((tq, tk), F32),
                          pltpu.VMEM((tq, tk), BF16), pltpu.VMEM((tq, tk), BF16),
                          pltpu.VMEM((tq, 1), F32), pltpu.VMEM((tq, 1), F32),
                          pltpu.VMEM((tq, 2 * LANES), F32), pltpu.VMEM((tq, 1), F32)] * 2,
        compiler_params=_cparams(("arbitrary", "arbitrary", "arbitrary")),
        name="diff_ctx" if ctx_queries else "diff_lat",
    )(scal, p, *kv_args, subln_g)


def _roll_bf16(a, shift):
    return pltpu.roll(a.astype(F32), shift, 1).astype(BF16)


def _gqa_core(sink_ref, q_ref, kk, vv, mask, o_ref):
    tq = q_ref.shape[0]
    lane = lax.broadcasted_iota(jnp.int32, (1, LANES), 1)
    half_of_lane = lax.shift_right_logical(lane, 6)
    kk_sw, vv_sw = _roll_bf16(kk, 64), _roll_bf16(vv, 64)
    row = lax.broadcasted_iota(jnp.int32, (2 * tq, 1), 0)
    for kvh in range(GQA_KV_HEADS):
        pairs = (2 * kvh, 2 * kvh + 1)
        outs = []
        for hh in range(2):
            kmat = kk if kvh == hh else kk_sw
            vmat = vv if kvh == hh else vv_sw
            qp = [q_ref[:, p * LANES:(p + 1) * LANES] for p in pairs]
            qs = jnp.concatenate([jnp.where(half_of_lane == hh, x, jnp.zeros_like(x)) for x in qp], axis=0)
            s = _dot_nt(qs, kmat)
            if mask is not None:
                s = jnp.where(mask, s, NEG_BIG)
            sink = jnp.where(row < tq, sink_ref[2 * pairs[0] + hh], sink_ref[2 * pairs[1] + hh])
            m = jnp.maximum(jnp.max(s, axis=-1, keepdims=True), sink)
            pr = jnp.exp(s - m)
            den = jnp.sum(pr, axis=-1, keepdims=True) + jnp.exp(sink - m)
            outs.append(_dot(pr.astype(BF16), vmat) / den)
        for pi, p in enumerate(pairs):
            rs = slice(pi * tq, (pi + 1) * tq)
            o_ref[:, p * LANES:(p + 1) * LANES] = jnp.where(half_of_lane == 0, outs[0][rs], outs[1][rs]).astype(BF16)


def _gqa_lat_kernel(sink_ref, q_ref, kp_ref, ko_ref, kn_ref, kc_ref, vp_ref, vo_ref, vn_ref, vc_ref, o_ref, *, seq):
    tq = q_ref.shape[0]
    qi = pl.program_id(1)
    kk = jnp.concatenate([kp_ref[...], ko_ref[...], kn_ref[...], kc_ref[...]], axis=0)
    vv = jnp.concatenate([vp_ref[...], vo_ref[...], vn_ref[...], vc_ref[...]], axis=0)
    nk = kk.shape[0]
    n_loc = tq + 2 * WINDOW
    r2 = lax.broadcasted_iota(jnp.int32, (2 * tq, nk), 0)
    r = jnp.where(r2 >= tq, r2 - tq, r2)
    c = lax.broadcasted_iota(jnp.int32, (2 * tq, nk), 1)
    kpos = qi * tq - WINDOW + c
    mask = ((jnp.abs(r - c + WINDOW) <= WINDOW) & (kpos >= 0) & (kpos < seq)) | (c >= n_loc)
    _gqa_core(sink_ref, q_ref, kk, vv, mask, o_ref)


def _gqa_ctx_kernel(sink_ref, q_ref, kc_ref, vc_ref, o_ref):
    _gqa_core(sink_ref, q_ref, kc_ref[...], vc_ref[...], None, o_ref)


def _gqa(p, sink, dims, ctx_queries):
    b, s, c = dims["batch"], dims["seq"], dims["ctx"]
    qw = GQA_HEADS * GQA_DH
    qc0, kc0, vc0 = COL_GQA_Q // qw, COL_GQA_K // LANES, COL_GQA_V // LANES
    ctx_blk = (b * s) // c
    sm = pl.BlockSpec(memory_space=pltpu.SMEM)
    if ctx_queries:
        return pl.pallas_call(
            _gqa_ctx_kernel,
            grid=(b,),
            in_specs=[sm, pl.BlockSpec((c, qw), lambda bi: (ctx_blk + bi, qc0)),
                      pl.BlockSpec((c, LANES), lambda bi: (ctx_blk + bi, kc0)),
                      pl.BlockSpec((c, LANES), lambda bi: (ctx_blk + bi, vc0))],
            out_specs=pl.BlockSpec((c, qw), lambda bi: (bi, 0)),
            out_shape=jax.ShapeDtypeStruct((b * c, qw), BF16),
            compiler_params=_cparams(("arbitrary",)),
            name="gqa_ctx",
        )(sink, p, p, p)
    tq = GQA_TQ
    nq = s // tq
    wpq = tq // WINDOW
    nwb = s // WINDOW

    def loc_specs(col):
        return [pl.BlockSpec((WINDOW, LANES), lambda bi, qi: (bi * nwb + jnp.maximum(qi * wpq - 1, 0), col)),
                pl.BlockSpec((tq, LANES), lambda bi, qi: (bi * nq + qi, col)),
                pl.BlockSpec((WINDOW, LANES), lambda bi, qi: (bi * nwb + jnp.minimum(qi * wpq + wpq, nwb - 1), col)),
                pl.BlockSpec((c, LANES), lambda bi, qi: (ctx_blk + bi, col))]

    return pl.pallas_call(
        functools.partial(_gqa_lat_kernel, seq=s),
        grid=(b, nq),
        in_specs=[sm, pl.BlockSpec((tq, qw), lambda bi, qi: (bi * nq + qi, qc0))] + loc_specs(kc0) + loc_specs(vc0),
        out_specs=pl.BlockSpec((tq, qw), lambda bi, qi: (bi * nq + qi, 0)),
        out_shape=jax.ShapeDtypeStruct((b * s, qw), BF16),
        compiler_params=_cparams(("arbitrary", "arbitrary")),
        name="gqa_lat",
    )(sink, p, *([p] * 8))


def _merge_kernel(x_ref, yr_ref, ydl_ref, ydc_ref, ygl_ref, ygc_ref, gmix_ref, gffn_ref, sh1_ref, sc1_ref, ga1_ref,
                  sh2_ref, sc2_ref, wg_ref, wb_ref, wo_ref, wrh_ref, wrl_ref, br_ref,
                  x1_ref, h2_ref, ti_ref, tw_ref, *, n_lat_tiles):
    d = x_ref.shape[1]
    x = x_ref[...]
    h = (_rms(x, gmix_ref[...]) * (1.0 + sc1_ref[0]) + sh1_ref[0]).astype(BF16)
    rw = RET_HEADS * RET_DV
    dw = DIFF_HEADS * LANES
    is_lat = pl.program_id(0) < n_lat_tiles
    yd = jnp.where(is_lat, ydl_ref[...], ydc_ref[...])
    yg = jnp.where(is_lat, ygl_ref[...], ygc_ref[...])
    merged = None
    for bi, (y, r0, r1) in enumerate(((yr_ref[...], 0, rw), (yd, rw, rw + dw), (yg, rw + dw, wb_ref.shape[0]))):
        gate = _sigmoid(_dot(h, wg_ref[:, bi * d:(bi + 1) * d]))
        term = gate * _dot(y, wb_ref[r0:r1, :])
        merged = term if merged is None else merged + term
    x1 = x + ga1_ref[0] * _dot(merged.astype(BF16), wo_ref[...])
    x1_ref[...] = x1
    h2 = _rms(x1, gffn_ref[...]) * (1.0 + sc2_ref[0]) + sh2_ref[0]
    tm, ns = x_ref.shape[0], d // LANES
    for s in range(ns):
        h2_ref[pl.ds(s, tm, stride=ns), :] = h2[:, s * LANES:(s + 1) * LANES]
    hh, hl = _split_bf16(h2)
    wrh, wrl = wrh_ref[...], wrl_ref[...]
    logits = _dot(hh, wrh) + (_dot(hl, wrh) + _dot(hh, wrl)) + br_ref[...]
    eidx = lax.broadcasted_iota(jnp.int32, logits.shape, 1).astype(F32)
    vals = []
    for k in range(TOP_K):
        mx = jnp.max(logits, axis=1, keepdims=True)
        ix = jnp.min(jnp.where(logits == mx, eidx, float(N_EXPERTS)), axis=1, keepdims=True)
        vals.append(mx)
        ti_ref[:, k:k + 1] = ix.astype(jnp.int32)
        logits = jnp.where(eidx == ix, -jnp.inf, logits)
    es = [jnp.exp(v - vals[0]) for v in vals]
    tot = es[0] + es[1] + es[2] + es[3]
    for k in range(TOP_K):
        tw_ref[:, k:k + 1] = es[k] / tot


def _merge(xall, yr, yd, yg, gmix, gffn, mod3, wg, wb, wo, wrh, wrl, br, dims, ntok):
    d = xall.shape[1]
    tm = dims["tm_merge"]
    tps = dims["seq"] // tm
    n_lat_tiles = dims["batch"] * tps

    def mrow(i):
        return jnp.minimum(i // tps, dims["batch"])

    def modspec(k):
        return pl.BlockSpec((1, 1, d), lambda i: (mrow(i), 0, k))

    def full(a):
        return pl.BlockSpec(a.shape, lambda i: (0,) * a.ndim)

    def rows(w):
        return pl.BlockSpec((tm, w), lambda i: (i, 0))

    def lat_ctx(pair):
        lat, ctx = pair
        w = lat.shape[1]
        lat_spec = pl.BlockSpec((tm, w), lambda i: (jnp.minimum(i, n_lat_tiles - 1), 0))
        if ctx is None:
            return [lat_spec, lat_spec], [lat, lat]
        n_ctx_tiles = ctx.shape[0] // tm
        ctx_spec = pl.BlockSpec((tm, w), lambda i: (jnp.clip(i - n_lat_tiles, 0, n_ctx_tiles - 1), 0))
        return [lat_spec, ctx_spec], [lat, ctx]

    yd_specs, yd_args = lat_ctx(yd)
    yg_specs, yg_args = lat_ctx(yg)
    return pl.pallas_call(
        functools.partial(_merge_kernel, n_lat_tiles=n_lat_tiles),
        grid=(ntok // tm,),
        in_specs=[rows(d), rows(yr.shape[1])] + yd_specs + yg_specs + [full(gmix), full(gffn),
                  modspec(0), modspec(1), modspec(2), modspec(3), modspec(4),
                  full(wg), full(wb), full(wo), full(wrh), full(wrl), full(br)],
        out_specs=[rows(d), pl.BlockSpec((tm * (d // LANES), LANES), lambda i: (i, 0)),
                   pl.BlockSpec((tm, TOP_K), lambda i: (i, 0)), pl.BlockSpec((tm, TOP_K), lambda i: (i, 0))],
        out_shape=[jax.ShapeDtypeStruct((ntok, d), F32), jax.ShapeDtypeStruct((ntok * (d // LANES), LANES), F32),
                   jax.ShapeDtypeStruct((ntok, TOP_K), jnp.int32), jax.ShapeDtypeStruct((ntok, TOP_K), F32)],
        compiler_params=_cparams(("arbitrary",)),
        name="merge",
    )(xall, yr, *yd_args, *yg_args, gmix, gffn, mod3, mod3, mod3, mod3, mod3, wg, wb, wo, wrh, wrl, br)


TOK_BITS = 15
FFN_CHUNK = 256
GATHER_DEPTH = 3


def _moe_kernel(be_ref, nv_ref, nu_ref, base_ref, code_ref, h2_hbm, wgu_ref, bgu_ref, wdn_ref, bdn_ref, o_hbm,
                xbuf, ybuf, wgu_bf, wdn_bf, gsem, ssem):
    i = pl.program_id(0)
    nu = nu_ref[0]
    slot = i % 2
    gslot = lax.rem(i, GATHER_DEPTH)
    last_blk = pl.num_programs(0) - 1
    nb = MOE_BLOCK
    d = wdn_bf.shape[0]
    ns = d // LANES

    def gather_rows(blk, sl, r0, r1):
        base = base_ref[blk]
        for r in range(r0, r1):
            tok = code_ref[base + r] & ((1 << TOK_BITS) - 1)
            pltpu.make_async_copy(h2_hbm.at[pl.ds(pl.multiple_of(tok * ns, ns), ns)], xbuf.at[sl, pl.ds(r * ns, ns)],
                                  gsem.at[sl]).start(priority=r % 2)

    def wait_rows(n, copy_of_rows):
        @pl.when(n == nb)
        def _():
            copy_of_rows(nb).wait()

        @pl.when(n != nb)
        def _():
            def body(r, carry):
                copy_of_rows(1).wait()
                return carry
            lax.fori_loop(0, n, body, 0)

    def gather_wait(sl):
        pltpu.make_async_copy(h2_hbm.at[pl.ds(0, nb * ns)], xbuf.at[sl], gsem.at[sl]).wait()

    def scatter_rows(blk, sl, r0, r1):
        base = base_ref[blk]
        n = nv_ref[blk]

        def body(r):
            dst = lax.shift_right_logical(code_ref[base + r], TOK_BITS)
            pltpu.make_async_copy(ybuf.at[sl, pl.ds(pl.multiple_of(r * ns, ns), ns)],
                                  o_hbm.at[pl.ds(pl.multiple_of(dst * ns, ns), ns)], ssem.at[sl]
                                  ).start(priority=r % 2 if isinstance(r, int) else 0)

        @pl.when(n == nb)
        def _():
            for r in range(r0, r1):
                body(r)

        @pl.when(n != nb)
        def _():
            def step(r, carry):
                body(r)
                return carry
            lax.fori_loop(r0, jnp.clip(n, r0, r1), step, 0)

    def scatter_wait(blk, sl):
        wait_rows(nv_ref[blk], lambda n: pltpu.make_async_copy(
            ybuf.at[sl, pl.ds(0, n * ns)], o_hbm.at[pl.ds(0, n * ns)], ssem.at[sl]))

    @pl.when(i < nu)
    def _():
        @pl.when(i == 0)
        def _():
            gather_rows(0, 0, 0, nb)
            gather_rows(jnp.minimum(1, last_blk), 1, 0, nb)

        @pl.when((i == 0) | (be_ref[i] != be_ref[jnp.maximum(i - 1, 0)]))
        def _():
            rc = 128

            def cast(r, carry):
                r0 = pl.multiple_of(r * rc, rc)
                wgu_bf[pl.ds(r0, rc), :] = wgu_ref[0, 0, pl.ds(r0, rc), :].astype(BF16)
                wdn_bf[pl.ds(r0, rc), :] = wdn_ref[0, 0, pl.ds(r0, rc), :].astype(BF16)
                return carry
            lax.fori_loop(0, d // rc, cast, 0)

        gather_wait(gslot)
        nxt = jnp.minimum(i + GATHER_DEPTH - 1, last_blk)
        nslot = lax.rem(i + GATHER_DEPTH - 1, GATHER_DEPTH)
        x = jnp.concatenate([xbuf[gslot, pl.ds(s, nb, stride=ns), :] for s in range(ns)], axis=1).astype(BF16)
        cw = FFN_CHUNK
        nch = d // cw
        y = None
        for ci in range(nch):
            c0, c1 = ci * cw, (ci + 1) * cw
            gather_rows(nxt, nslot, ci * (nb // nch), (ci + 1) * (nb // nch))
            glu = jnp.minimum(_dot(x, wgu_bf[:, c0:c1]) + bgu_ref[0, 0, :, c0:c1], SWIGLU_LIMIT)
            lin = jnp.clip(_dot(x, wgu_bf[:, d + c0:d + c1]) + bgu_ref[0, 0, :, d + c0:d + c1],
                           -SWIGLU_LIMIT, SWIGLU_LIMIT)
            act = glu * _sigmoid(SWIGLU_ALPHA * glu) * (lin + 1.0)
            part = _dot(act.astype(BF16), wdn_bf[c0:c1, :])
            y = part if y is None else y + part
        y = y + bdn_ref[0, 0]

        @pl.when(i >= 2)
        def _():
            scatter_wait(i - 2, slot)

        for s in range(ns):
            ybuf[slot, pl.ds(s, nb, stride=ns), :] = y[:, s * LANES:(s + 1) * LANES]
        scatter_rows(i, slot, 0, nb)

        @pl.when(i == nu - 1)
        def _():
            gather_wait(lax.rem(i + 1, GATHER_DEPTH))
            gather_wait(nslot)

            @pl.when(i >= 1)
            def _():
                scatter_wait(i - 1, 1 - slot)
            scatter_wait(i, slot)


def _moe(h2, route, w_gu, b_gu, w_dn, b_dn, layer):
    codes, block_expert, n_valid, n_used, base = route
    d = w_dn.shape[-1]
    ns = d // LANES
    ntok = h2.shape[0] // ns
    n_blocks = block_expert.shape[0]
    depth, ne = w_gu.shape[:2]

    def wspec(rows, cols):
        return pl.BlockSpec((1, 1, rows, cols), lambda i, be, nv, nu, bs, cd: (layer, be[i], 0, 0))

    grid_spec = pltpu.PrefetchScalarGridSpec(
        num_scalar_prefetch=5,
        grid=(n_blocks,),
        in_specs=[pl.BlockSpec(memory_space=pl.ANY),
                  wspec(d, 2 * d), wspec(1, 2 * d), wspec(d, d), wspec(1, d)],
        out_specs=pl.BlockSpec(memory_space=pl.ANY),
        scratch_shapes=[pltpu.VMEM((GATHER_DEPTH, MOE_BLOCK * ns, LANES), F32),
                        pltpu.VMEM((2, MOE_BLOCK * ns, LANES), F32),
                        pltpu.VMEM((d, 2 * d), BF16), pltpu.VMEM((d, d), BF16),
                        pltpu.SemaphoreType.DMA((GATHER_DEPTH,)), pltpu.SemaphoreType.DMA((2,))])
    return pl.pallas_call(
        _moe_kernel,
        grid_spec=grid_spec,
        out_shape=jax.ShapeDtypeStruct((TOP_K * ntok * ns, LANES), F32),
        compiler_params=_cparams(("arbitrary",)),
        name="moe",
    )(block_expert, n_valid, n_used, base, codes, h2, w_gu, b_gu.reshape(depth, ne, 1, 2 * d), w_dn,
      b_dn.reshape(depth, ne, 1, d))


def _route(ti, ntok):
    a = TOP_K * ntok
    n_blocks = (a + N_EXPERTS * (MOE_BLOCK - 1) + MOE_BLOCK - 1) // MOE_BLOCK
    expert = ti.T.reshape(a)
    aid_bits = (a - 1).bit_length()
    assert aid_bits + 5 < 32 and N_EXPERTS <= 32
    aid = jnp.arange(a, dtype=jnp.int32)
    aid_sorted = jnp.sort(lax.shift_left(expert, aid_bits) | aid) & ((1 << aid_bits) - 1)
    codes = lax.shift_left(aid_sorted, TOK_BITS) | (aid_sorted % ntok)
    codes = jnp.concatenate([codes, jnp.zeros((MOE_BLOCK,), jnp.int32)])
    counts = jnp.sum((expert[:, None] == jnp.arange(N_EXPERTS, dtype=jnp.int32)[None, :]).astype(jnp.int32), axis=0)
    padded = (counts + MOE_BLOCK - 1) // MOE_BLOCK * MOE_BLOCK
    pad_end = jnp.cumsum(padded)
    pad_start = pad_end - padded
    cstart = jnp.cumsum(counts) - counts
    n_used = (pad_end[-1:] // MOE_BLOCK).astype(jnp.int32)
    block_start = jnp.arange(n_blocks, dtype=jnp.int32) * MOE_BLOCK
    block_expert = jnp.minimum(jnp.sum((pad_end[None, :] <= block_start[:, None]).astype(jnp.int32), axis=1),
                               N_EXPERTS - 1)
    n_valid = jnp.clip(counts[block_expert] - (block_start - pad_start[block_expert]), 0, MOE_BLOCK).astype(jnp.int32)
    base = jnp.clip(block_start - (pad_start - cstart)[block_expert], 0, a).astype(jnp.int32)
    return codes, block_expert, n_valid, n_used, base


def _combine_kernel(x1_ref, o0_ref, o1_ref, o2_ref, o3_ref, tw_ref, ga2_ref, gf_ref, x2_ref, *, final):
    o_refs = (o0_ref, o1_ref, o2_ref, o3_ref)
    tm, ns = x1_ref.shape[0], x1_ref.shape[1] // LANES
    w = [jnp.broadcast_to(tw_ref[:, k:k + 1], (tm, LANES)) for k in range(TOP_K)]
    for s in range(ns):
        cols = slice(s * LANES, (s + 1) * LANES)
        y = None
        for k in range(TOP_K):
            term = w[k] * o_refs[k][pl.ds(s, tm, stride=ns), :]
            y = term if y is None else y + term
        x2_ref[:, cols] = x1_ref[:, cols] + ga2_ref[0, :, cols] * y
    if final:
        x2_ref[...] = _rms(x2_ref[...], gf_ref[...])


def _combine(x1, o4, tw, mod3, g_final, dims, final):
    ntok, d = x1.shape
    tm = dims["tm_merge"]
    tps = dims["seq"] // tm
    nt = ntok // tm

    def ospec(k):
        return pl.BlockSpec((tm * (d // LANES), LANES), lambda i: (k * nt + i, 0))

    return pl.pallas_call(
        functools.partial(_combine_kernel, final=final),
        grid=(nt,),
        in_specs=[pl.BlockSpec((tm, d), lambda i: (i, 0)), ospec(0), ospec(1), ospec(2), ospec(3),
                  pl.BlockSpec((tm, TOP_K), lambda i: (i, 0)),
                  pl.BlockSpec((1, 1, d), lambda i: (jnp.minimum(i // tps, dims["batch"]), 0, 5)),
                  pl.BlockSpec((1, d), lambda i: (0, 0))],
        out_specs=pl.BlockSpec((tm, d), lambda i: (i, 0)),
        out_shape=jax.ShapeDtypeStruct((ntok, d), F32),
        compiler_params=_cparams(("arbitrary",)),
        name="combine",
    )(x1, o4, o4, o4, o4, tw, mod3, g_final)


def _rope_tables(seq, pad_rows):
    pos = jnp.arange(seq, dtype=jnp.int32)
    row = (pos // GRID_W).astype(F32)
    col = (pos % GRID_W).astype(F32)

    def cs(head_dim):
        quarter = head_dim // 4
        inv_freq = ROPE_BASE ** (-jnp.arange(quarter, dtype=F32) / quarter)
        ang = jnp.concatenate([row[:, None] * inv_freq, col[:, None] * inv_freq], axis=-1)
        return jnp.cos(ang), jnp.sin(ang)

    c128, s128 = cs(RET_DK)
    c64, s64 = cs(DIFF_DK)
    ones = jnp.ones((pad_rows, LANES), F32)
    zeros = jnp.zeros((pad_rows, LANES), F32)
    a128 = jnp.concatenate([jnp.concatenate([c128, c128], -1), ones], 0)
    b128 = jnp.concatenate([jnp.concatenate([-s128, s128], -1), zeros], 0)
    a64 = jnp.concatenate([jnp.concatenate([c64, c64, c64, c64], -1), ones], 0)
    b64 = jnp.concatenate([jnp.concatenate([-s64, s64, -s64, s64], -1), zeros], 0)
    return a128, b128, a64, b64


def _pick_tile(cands, *sizes):
    for t in cands:
        if all(s % t == 0 for s in sizes):
            return t
    raise ValueError("no tile size fits")


def kernel(x, c, ctx, c_ctx, w_mod, b_mod, g_norm_mix, g_norm_ffn, w_in, ret_decay_fwd, ret_decay_bwd, diff_lambda_q1, diff_lambda_k1, diff_lambda_q2, diff_lambda_k2, diff_subln_g, gqa_sink, w_branch, w_out, w_router, b_router, w_gate_up, b_gate_up, w_down, b_down, g_final):
    b, s, d = x.shape
    n_ctx = ctx.shape[1]
    depth = w_mod.shape[0]
    n_lat = b * s
    tt = n_lat + b * n_ctx
    assert s % RET_CHUNK == 0 and n_ctx % RET_CHUNK == 0 and s % GQA_TQ == 0 and n_lat % n_ctx == 0
    assert tt < (1 << TOK_BITS) and b < 8
    tm = _pick_tile((1024, 512, 256), s, b * n_ctx)
    tm_merge = _pick_tile((512, 256), s, b * n_ctx)
    dims = dict(batch=b, seq=s, ctx=n_ctx, tm=tm, tm_merge=tm_merge,
                n_lat_tiles=n_lat // tm, tiles_per_seq=s // tm)

    xall = jnp.concatenate([x.reshape(n_lat, d), ctx.reshape(b * n_ctx, d)], axis=0)
    cvec = jnp.concatenate([c, c_ctx[None, :], jnp.zeros((8 - b - 1, d), F32)], axis=0)
    mod = _mod_all(cvec, w_mod, b_mod)
    tabs = _rope_tables(s, tm)

    for l in range(depth):
        last = l == depth - 1
        mod3 = mod[l].reshape(8, 1, 6 * d)
        w_l = w_in[l]
        p = _inproj(xall, g_norm_mix[l][None, :], mod3, w_in, l, tabs, dims)

        lg = jnp.stack([jax.nn.log_sigmoid(ret_decay_fwd[l].astype(F32)),
                        jax.nn.log_sigmoid(ret_decay_bwd[l].astype(F32))], axis=0)
        y_r = _retention(p, lg, dims)

        lam_init = 0.8 - 0.6 * math.exp(-0.3 * l)
        lam = (jnp.exp(jnp.sum(diff_lambda_q1[l].astype(F32) * diff_lambda_k1[l].astype(F32)))
               - jnp.exp(jnp.sum(diff_lambda_q2[l].astype(F32) * diff_lambda_k2[l].astype(F32))) + lam_init)
        scal = jnp.stack([lam, jnp.asarray(1.0 - lam_init, F32)])
        subln = diff_subln_g[l][None, :].astype(F32)
        sink = gqa_sink[l].astype(F32)
        y_d = (_diff_attention(p, scal, subln, dims, False),
               None if last else _diff_attention(p, scal, subln, dims, True))
        y_g = (_gqa(p, sink, dims, False), None if last else _gqa(p, sink, dims, True))
        ntok = n_lat if last else tt

        wrh, wrl = _split_bf16(w_router[l])
        x1, h2, ti, tw = _merge(xall, y_r, y_d, y_g, g_norm_mix[l][None, :], g_norm_ffn[l][None, :], mod3,
                                w_l[:, P_COLS:].astype(BF16), w_branch[l].astype(BF16), w_out[l].astype(BF16),
                                wrh, wrl, b_router[l][None, :].astype(F32), dims, ntok)
        o4 = _moe(h2, _route(ti, ntok), w_gate_up, b_gate_up, w_down, b_down, l)
        xall = _combine(x1, o4, tw, mod3, g_final[None, :], dims, last)
    return xall.reshape(b, s, d)
```

```python
import functools
import math

import jax
import jax.numpy as jnp
from jax import lax
from jax.experimental import pallas as pl
from jax.experimental.pallas import tpu as pltpu

F32 = jnp.float32
BF16 = jnp.bfloat16

GRID_W = 64
RET_HEADS, RET_DK, RET_DV = 4, 128, 256
DIFF_HEADS, DIFF_DK = 4, 64
GQA_HEADS, GQA_KV_HEADS, GQA_DH = 8, 2, 64
WINDOW = 128
N_EXPERTS, TOP_K = 32, 4
SWIGLU_ALPHA, SWIGLU_LIMIT = 1.702, 7.0
MOE_BLOCK = 256
ROPE_BASE = 10000.0
NORM_EPS = 1e-6
NEG_BIG = -1e30

LANES = 128
P_COLS = 5376
IN_TN = 768
RET_CHUNK = 256
DIFF_TQ, DIFF_TK = 1024, 512
DIFF_ROW_CHUNK = 64
LOG2E = math.log2(math.e)
GQA_TQ = 256
VMEM_LIMIT = 56 * 1024 * 1024

_PLAIN = ("plain", 1.0)
_GROUP_KINDS = ([("r128", RET_DK ** -0.5)] * 4 + [("r128", 1.0)] * 4 + [_PLAIN] * 16
                + [("r64", DIFF_DK ** -0.5 * LOG2E)] * 4 + [("r64", 1.0)] * 4 + [_PLAIN] * 4
                + [("r64", GQA_DH ** -0.5)] * 4 + [("r64", 1.0)] + [_PLAIN])
COL_RET_Q, COL_RET_K, COL_RET_V, COL_RET_G = 0, 512, 1024, 2048
COL_DIFF_Q, COL_DIFF_K, COL_DIFF_V = 3072, 3584, 4096
COL_GQA_Q, COL_GQA_K, COL_GQA_V = 4608, 5120, 5248


def _cparams(sem):
    return pltpu.CompilerParams(dimension_semantics=sem, vmem_limit_bytes=VMEM_LIMIT)


def _sigmoid(x):
    return 1.0 / (1.0 + jnp.exp(-x))


def _dot(a, b):
    return jnp.dot(a, b, preferred_element_type=F32)


def _dot_nt(a, b):
    return lax.dot_general(a, b, (((1,), (1,)), ((), ())), preferred_element_type=F32)


def _dot_tn(a, b):
    return lax.dot_general(a, b, (((0,), (0,)), ((), ())), preferred_element_type=F32)


def _split_bf16(a):
    hi = a.astype(BF16)
    lo = (a - hi.astype(F32)).astype(BF16)
    return hi, lo


def _rms(x, g):
    return x * lax.rsqrt(jnp.mean(x * x, axis=-1, keepdims=True) + NORM_EPS) * g


def _mod_kernel(c_ref, w_ref, b_ref, o_ref):
    c = c_ref[...]
    s = c * _sigmoid(c)
    sh, sl = _split_bf16(s)
    wh, wl = _split_bf16(w_ref[0])
    o_ref[0] = _dot(sh, wh) + (_dot(sl, wh) + _dot(sh, wl)) + b_ref[0]


def _mod_all(cvec, w_mod, b_mod):
    depth, d, n = w_mod.shape
    tn = n // 4
    return pl.pallas_call(
        _mod_kernel,
        grid=(depth, n // tn),
        in_specs=[pl.BlockSpec((8, d), lambda l, j: (0, 0)),
                  pl.BlockSpec((1, d, tn), lambda l, j: (l, 0, j)),
                  pl.BlockSpec((1, 1, tn), lambda l, j: (l, 0, j))],
        out_specs=pl.BlockSpec((1, 8, tn), lambda l, j: (l, 0, j)),
        out_shape=jax.ShapeDtypeStruct((depth, 8, n), F32),
        compiler_params=_cparams(("arbitrary", "arbitrary")),
        name="mod",
    )(cvec, w_mod, b_mod.reshape(depth, 1, n))


def _rope_group(a, kind, scale, a128, b128, a64, b64):
    if kind == "plain":
        return a
    if kind == "r128":
        out = a * a128 + pltpu.roll(a, 64, 1) * b128
    else:
        lane = lax.broadcasted_iota(jnp.int32, (1, LANES), 1)
        first = (lane & 63) < 32
        partner = jnp.where(first, pltpu.roll(a, 96, 1), pltpu.roll(a, 32, 1))
        out = a * a64 + partner * b64
    return out * scale if scale != 1.0 else out


def _inproj_kernel(x_ref, g_ref, sc_ref, sh_ref, w_ref, a128_ref, b128_ref, a64_ref, b64_ref,
                   o_ref, h_scr, acc0_scr, acc1_scr, *, n_tiles):
    j = pl.program_id(1)
    accs = (acc0_scr, acc1_scr)

    @pl.when(j == 0)
    def _():
        y = _rms(x_ref[...], g_ref[...])
        h_scr[...] = (y * (1.0 + sc_ref[0]) + sh_ref[0]).astype(BF16)

    gpt = IN_TN // LANES
    for jj in range(n_tiles + 1):
        @pl.when(j == jj)
        def _(jj=jj):
            if jj < n_tiles:
                accs[jj % 2][...] = _dot(h_scr[...], w_ref[0].astype(BF16))
            if jj >= 1:
                acc = accs[(jj - 1) % 2]
                for g in range(gpt):
                    kind, scale = _GROUP_KINDS[(jj - 1) * gpt + g]
                    a = acc[:, g * LANES:(g + 1) * LANES]
                    out = _rope_group(a, kind, scale, a128_ref[...], b128_ref[...], a64_ref[...], b64_ref[...])
                    o_ref[:, g * LANES:(g + 1) * LANES] = out.astype(BF16)


def _inproj(xall, g, mod3, w_in, layer, tabs, dims):
    tt, d = xall.shape
    tm = dims["tm"]
    n_lat_tiles, tps = dims["n_lat_tiles"], dims["tiles_per_seq"]
    n_tiles = P_COLS // IN_TN

    def mrow(i):
        return jnp.minimum(i // tps, dims["batch"])

    def trow(i):
        return jnp.where(i < n_lat_tiles, i % tps, tps)

    tab_spec = pl.BlockSpec((tm, LANES), lambda i, j: (trow(i), 0))
    return pl.pallas_call(
        functools.partial(_inproj_kernel, n_tiles=n_tiles),
        grid=(tt // tm, n_tiles + 1),
        in_specs=[pl.BlockSpec((tm, d), lambda i, j: (i, 0)),
                  pl.BlockSpec((1, d), lambda i, j: (0, 0)),
                  pl.BlockSpec((1, 1, d), lambda i, j: (mrow(i), 0, 1)),
                  pl.BlockSpec((1, 1, d), lambda i, j: (mrow(i), 0, 0)),
                  pl.BlockSpec((1, d, IN_TN), lambda i, j: (layer, 0, jnp.minimum(j, n_tiles - 1))),
                  tab_spec, tab_spec, tab_spec, tab_spec],
        out_specs=pl.BlockSpec((tm, IN_TN), lambda i, j: (i, jnp.maximum(j - 1, 0))),
        out_shape=jax.ShapeDtypeStruct((tt, P_COLS), BF16),
        scratch_shapes=[pltpu.VMEM((tm, d), BF16), pltpu.VMEM((tm, IN_TN), F32), pltpu.VMEM((tm, IN_TN), F32)],
        compiler_params=_cparams(("arbitrary", "arbitrary")),
        name="inproj",
    )(xall, g, mod3, mod3, w_in, *tabs)


def _ret_kernel(lg_ref, q_ref, k_ref, v_ref, g_ref, y_ref, s_scr, of_scr, o_scr, *, ncc, ncl):
    ph = pl.program_id(1)
    t = pl.program_id(2)
    ch = RET_CHUNK

    @pl.when(t == 0)
    def _():
        s_scr[...] = jnp.zeros_like(s_scr)

    fwd = ph == 0
    ii = lax.broadcasted_iota(jnp.int32, (ch, ch), 0).astype(F32)
    jj = lax.broadcasted_iota(jnp.int32, (ch, ch), 1).astype(F32)
    rel = jnp.where(fwd, ii - jj, jj - ii)
    pos = lax.broadcasted_iota(jnp.int32, (ch, 1), 0).astype(F32)
    qpos = jnp.where(fwd, pos + 1.0, ch - pos)
    kpos = jnp.where(fwd, ch - 1.0 - pos, pos)
    mirror = jnp.where(t < ncc, ncc - 1 - t, ncc + ncl - 1 - (t - ncc))
    row0 = pl.multiple_of(jnp.where(fwd, t, mirror) * ch, ch)

    for h in range(RET_HEADS):
        lg = lg_ref[ph, h]
        dmask = jnp.where(rel >= 0.0, jnp.exp(jnp.maximum(rel, 0.0) * lg), 0.0)
        q = q_ref[:, h * RET_DK:(h + 1) * RET_DK]
        k = k_ref[:, h * RET_DK:(h + 1) * RET_DK]
        v = v_ref[:, h * RET_DV:(h + 1) * RET_DV]
        a = (_dot_nt(q, k) * dmask).astype(BF16)
        qd = (q.astype(F32) * jnp.exp(qpos * lg)).astype(BF16)
        kd = (k.astype(F32) * jnp.exp(kpos * lg)).astype(BF16)
        state = s_scr[h]
        o_scr[:, h * RET_DV:(h + 1) * RET_DV] = _dot(a, v) + _dot(qd, state.astype(BF16))
        s_scr[h] = state * jnp.exp(ch * lg) + _dot_tn(kd, v)

    @pl.when(fwd)
    def _():
        of_scr[pl.ds(row0, ch), :] = o_scr[...]

    @pl.when(jnp.logical_not(fwd))
    def _():
        for h in range(RET_HEADS):
            cs = slice(h * RET_DV, (h + 1) * RET_DV)
            ot = o_scr[:, cs] + of_scr[pl.ds(row0, ch), cs]
            mu = jnp.mean(ot, axis=-1, keepdims=True)
            oc = ot - mu
            var = jnp.mean(oc * oc, axis=-1, keepdims=True)
            on = oc * lax.rsqrt(var + NORM_EPS)
            gate = g_ref[:, cs].astype(F32)
            y_ref[:, cs] = (gate * _sigmoid(gate) * on).astype(BF16)


def _retention(p, lg, dims):
    tt = p.shape[0]
    ch = RET_CHUNK
    b, s, c = dims["batch"], dims["seq"], dims["ctx"]
    ncc, ncl = c // ch, s // ch
    nsteps = ncc + ncl
    ctx_base = (b * s) // ch

    def rblk(bi, ph, t):
        fwd_blk = jnp.where(t < ncc, ctx_base + bi * ncc + t, bi * ncl + (t - ncc))
        bwd_blk = jnp.where(t < ncc, ctx_base + bi * ncc + (ncc - 1 - t), bi * ncl + (ncl - 1 - (t - ncc)))
        return jnp.where(ph == 0, fwd_blk, bwd_blk)

    def oblk(bi, ph, t):
        return rblk(bi, 1, jnp.where(ph == 0, 0, t))

    qk_w = RET_HEADS * RET_DK
    v_w = RET_HEADS * RET_DV
    return pl.pallas_call(
        functools.partial(_ret_kernel, ncc=ncc, ncl=ncl),
        grid=(b, 2, nsteps),
        in_specs=[pl.BlockSpec(memory_space=pltpu.SMEM),
                  pl.BlockSpec((ch, qk_w), lambda bi, ph, t: (rblk(bi, ph, t), COL_RET_Q // qk_w)),
                  pl.BlockSpec((ch, qk_w), lambda bi, ph, t: (rblk(bi, ph, t), COL_RET_K // qk_w)),
                  pl.BlockSpec((ch, v_w), lambda bi, ph, t: (rblk(bi, ph, t), COL_RET_V // v_w)),
                  pl.BlockSpec((ch, v_w), lambda bi, ph, t: (rblk(bi, ph, t), COL_RET_G // v_w))],
        out_specs=pl.BlockSpec((ch, v_w), lambda bi, ph, t: (oblk(bi, ph, t), 0)),
        out_shape=jax.ShapeDtypeStruct((tt, v_w), BF16),
        scratch_shapes=[pltpu.VMEM((RET_HEADS, RET_DK, RET_DV), F32),
                        pltpu.VMEM((nsteps * ch, v_w), F32), pltpu.VMEM((ch, v_w), F32)],
        compiler_params=_cparams(("arbitrary", "arbitrary", "arbitrary")),
        name="retention",
    )(lg, p, p, p, p)


def _diff_kernel(sc_ref, q_ref, *refs, n_src, tk):
    kv = refs[:2 * n_src]
    g_ref, o_ref = refs[2 * n_src], refs[2 * n_src + 1]
    kx_scr, vx_scr = refs[2 * n_src + 2], refs[2 * n_src + 3]
    nbuf = 8
    streams = (refs[2 * n_src + 4:2 * n_src + 4 + nbuf], refs[2 * n_src + 4 + nbuf:2 * n_src + 4 + 2 * nbuf])
    tq = q_ref.shape[0]
    rc = DIFF_ROW_CHUNK
    n_keys = sum(kv[2 * si].shape[0] for si in range(n_src))
    n_tiles = kx_scr.shape[0] // tk

    @pl.when(pl.program_id(2) == 0)
    def _():
        vx_scr[:, LANES:] = jnp.ones((vx_scr.shape[0], LANES), BF16)
        r0 = 0
        for si in range(n_src):
            k_ref, v_ref = kv[2 * si], kv[2 * si + 1]
            kx_scr[r0:r0 + k_ref.shape[0], :] = k_ref[...]
            vx_scr[r0:r0 + v_ref.shape[0], :LANES] = v_ref[...]
            r0 += k_ref.shape[0]
        if r0 < kx_scr.shape[0]:
            kx_scr[r0:, :] = jnp.zeros((kx_scr.shape[0] - r0, LANES), BF16)
            vx_scr[r0:, :LANES] = jnp.zeros((kx_scr.shape[0] - r0, LANES), BF16)

    q = q_ref[...]
    lane = lax.broadcasted_iota(jnp.int32, (1, LANES), 1)
    zero = jnp.zeros_like(q)
    qs = (jnp.where(lane < DIFF_DK, q, zero), jnp.where(lane >= DIFF_DK, q, zero))
    S_BUF, P_BUF, AL_BUF, ACC, MAX = 0, 2, 4, 6, 7
    for bufs in streams:
        bufs[ACC][...] = jnp.zeros_like(bufs[ACC])
        bufs[MAX][...] = jnp.full(bufs[MAX].shape, NEG_BIG, F32)

    def tile_rows(t):
        return pl.ds(t * tk if isinstance(t, int) else pl.multiple_of(t * tk, tk), tk)

    def scores(t, par):
        kt = kx_scr[tile_rows(t), :]
        for st, bufs in enumerate(streams):
            bufs[S_BUF + par][...] = _dot_nt(qs[st], kt)

    def accumulate(t, par):
        vt = vx_scr[tile_rows(t), :]
        for bufs in streams:
            bufs[ACC][...] = bufs[AL_BUF + par][...] * bufs[ACC][...] + _dot(bufs[P_BUF + par][...], vt)

    def stage(t, par, prefetch, n_valid, drain_prev):
        if prefetch:
            scores(t + 1, 1 - par)
        if drain_prev:
            accumulate(t - 1, 1 - par)
        for bufs in streams:
            s_scr, p_scr, al_scr, m_scr = bufs[S_BUF + par], bufs[P_BUF + par], bufs[AL_BUF + par], bufs[MAX]
            m_all = m_scr[...]
            m_new, alpha = [], []
            for r in range(tq // rc):
                rows = slice(r * rc, (r + 1) * rc)
                s = s_scr[rows, :]
                if n_valid < tk:
                    col = lax.broadcasted_iota(jnp.int32, s.shape, 1)
                    s = jnp.where(col < n_valid, s, NEG_BIG)
                m_old = m_all[rows]
                mn = jnp.maximum(m_old, jnp.max(s, axis=-1, keepdims=True))
                p_scr[rows, :] = jnp.exp2(s - mn).astype(BF16)
                alpha.append(jnp.exp2(m_old - mn))
                m_new.append(mn)
            m_scr[...] = jnp.concatenate(m_new, axis=0)
            al_scr[...] = jnp.concatenate(alpha, axis=0)

    def n_valid_of(t):
        return n_keys - t * tk if t == n_tiles - 1 else tk

    scores(0, 0)
    stage(0, 0, n_tiles > 1, n_valid_of(0), False)
    pairs = max(0, (n_tiles - 2) // 2)
    if pairs > 0:
        def body(u, carry):
            stage(2 * u + 1, 1, True, tk, True)
            stage(2 * u + 2, 0, True, tk, True)
            return carry
        lax.fori_loop(0, pairs, body, 0)
    for t in range(2 * pairs + 1, n_tiles):
        stage(t, t % 2, t < n_tiles - 1, n_valid_of(t), True)
    accumulate(n_tiles - 1, (n_tiles - 1) % 2)
    lam = sc_ref[0]
    a1, a2 = streams[0][ACC], streams[1][ACC]
    o = a1[:, :LANES] / a1[:, LANES:] - lam * (a2[:, :LANES] / a2[:, LANES:])
    o = o * lax.rsqrt(jnp.mean(o * o, axis=-1, keepdims=True) + NORM_EPS) * g_ref[...] * sc_ref[1]
    o_ref[...] = o.astype(BF16)


def _diff_attention(p, scal, subln_g, dims, ctx_queries):
    b, s, c = dims["batch"], dims["seq"], dims["ctx"]
    tt = p.shape[0]
    qc0, kc0, vc0 = COL_DIFF_Q // LANES, COL_DIFF_K // LANES, COL_DIFF_V // LANES
    ctx_blk = (b * s) // c
    kctx = pl.BlockSpec((c, LANES), lambda bi, h, qi: (ctx_blk + bi, kc0 + h))
    vctx = pl.BlockSpec((c, LANES), lambda bi, h, qi: (ctx_blk + bi, vc0 + h))
    if ctx_queries:
        tq, nq = c, 1
        q_spec = pl.BlockSpec((tq, LANES), lambda bi, h, qi: (ctx_blk + bi, qc0 + h))
        o_spec = pl.BlockSpec((tq, LANES), lambda bi, h, qi: (bi, h))
        out_rows = b * c
        kv_specs, n_src, kv_args = [kctx, vctx], 1, (p, p)
    else:
        tq = min(DIFF_TQ, s)
        nq = s // tq
        q_spec = pl.BlockSpec((tq, LANES), lambda bi, h, qi: (bi * nq + qi, qc0 + h))
        o_spec = pl.BlockSpec((tq, LANES), lambda bi, h, qi: (bi * nq + qi, h))
        out_rows = b * s
        klat = pl.BlockSpec((s, LANES), lambda bi, h, qi: (bi, kc0 + h))
        vlat = pl.BlockSpec((s, LANES), lambda bi, h, qi: (bi, vc0 + h))
        kv_specs, n_src, kv_args = [klat, vlat, kctx, vctx], 2, (p, p, p, p)
    del tt
    n_keys = c if ctx_queries else s + c
    tk = min(DIFF_TK, n_keys)
    n_pad = -(-n_keys // tk) * tk
    return pl.pallas_call(
        functools.partial(_diff_kernel, n_src=n_src, tk=tk),
        grid=(b, DIFF_HEADS, nq),
        in_specs=[pl.BlockSpec(memory_space=pltpu.SMEM), q_spec] + kv_specs
                 + [pl.BlockSpec((1, LANES), lambda bi, h, qi: (0, 0))],
        out_specs=o_spec,
        out_shape=jax.ShapeDtypeStruct((out_rows, DIFF_HEADS * LANES), BF16),
        scratch_shapes=[pltpu.VMEM((n_pad, LANES), BF16), pltpu.VMEM((n_pad, 2 * LANES), BF16)]
                       + [pltpu.VMEM((tq, tk), F32), pltpu.VMEM((tq, tk), F32),
                          pltpu.VMEM((tq, tk), BF16), pltpu.VMEM((tq, tk), BF16),
                          pltpu.VMEM((tq, 1), F32), pltpu.VMEM((tq, 1), F32),
                          pltpu.VMEM((tq, 2 * LANES), F32), pltpu.VMEM((tq, 1), F32)] * 2,
        compiler_params=_cparams(("arbitrary", "arbitrary", "arbitrary")),
        name="diff_ctx" if ctx_queries else "diff_lat",
    )(scal, p, *kv_args, subln_g)


def _roll_bf16(a, shift):
    return pltpu.roll(a.astype(F32), shift, 1).astype(BF16)


def _gqa_core(sink_ref, q_ref, kk, vv, mask, o_ref):
    tq = q_ref.shape[0]
    lane = lax.broadcasted_iota(jnp.int32, (1, LANES), 1)
    half_of_lane = lax.shift_right_logical(lane, 6)
    kk_sw, vv_sw = _roll_bf16(kk, 64), _roll_bf16(vv, 64)
    row = lax.broadcasted_iota(jnp.int32, (2 * tq, 1), 0)
    for kvh in range(GQA_KV_HEADS):
        pairs = (2 * kvh, 2 * kvh + 1)
        outs = []
        for hh in range(2):
            kmat = kk if kvh == hh else kk_sw
            vmat = vv if kvh == hh else vv_sw
            qp = [q_ref[:, p * LANES:(p + 1) * LANES] for p in pairs]
            qs = jnp.concatenate([jnp.where(half_of_lane == hh, x, jnp.zeros_like(x)) for x in qp], axis=0)
            s = _dot_nt(qs, kmat)
            if mask is not None:
                s = jnp.where(mask, s, NEG_BIG)
            sink = jnp.where(row < tq, sink_ref[2 * pairs[0] + hh], sink_ref[2 * pairs[1] + hh])
            m = jnp.maximum(jnp.max(s, axis=-1, keepdims=True), sink)
            pr = jnp.exp(s - m)
            den = jnp.sum(pr, axis=-1, keepdims=True) + jnp.exp(sink - m)
            outs.append(_dot(pr.astype(BF16), vmat) / den)
        for pi, p in enumerate(pairs):
            rs = slice(pi * tq, (pi + 1) * tq)
            o_ref[:, p * LANES:(p + 1) * LANES] = jnp.where(half_of_lane == 0, outs[0][rs], outs[1][rs]).astype(BF16)


def _gqa_lat_kernel(sink_ref, q_ref, kp_ref, ko_ref, kn_ref, kc_ref, vp_ref, vo_ref, vn_ref, vc_ref, o_ref, *, seq):
    tq = q_ref.shape[0]
    qi = pl.program_id(1)
    kk = jnp.concatenate([kp_ref[...], ko_ref[...], kn_ref[...], kc_ref[...]], axis=0)
    vv = jnp.concatenate([vp_ref[...], vo_ref[...], vn_ref[...], vc_ref[...]], axis=0)
    nk = kk.shape[0]
    n_loc = tq + 2 * WINDOW
    r2 = lax.broadcasted_iota(jnp.int32, (2 * tq, nk), 0)
    r = jnp.where(r2 >= tq, r2 - tq, r2)
    c = lax.broadcasted_iota(jnp.int32, (2 * tq, nk), 1)
    kpos = qi * tq - WINDOW + c
    mask = ((jnp.abs(r - c + WINDOW) <= WINDOW) & (kpos >= 0) & (kpos < seq)) | (c >= n_loc)
    _gqa_core(sink_ref, q_ref, kk, vv, mask, o_ref)


def _gqa_ctx_kernel(sink_ref, q_ref, kc_ref, vc_ref, o_ref):
    _gqa_core(sink_ref, q_ref, kc_ref[...], vc_ref[...], None, o_ref)


def _gqa(p, sink, dims, ctx_queries):
    b, s, c = dims["batch"], dims["seq"], dims["ctx"]
    qw = GQA_HEADS * GQA_DH
    qc0, kc0, vc0 = COL_GQA_Q // qw, COL_GQA_K // LANES, COL_GQA_V // LANES
    ctx_blk = (b * s) // c
    sm = pl.BlockSpec(memory_space=pltpu.SMEM)
    if ctx_queries:
        return pl.pallas_call(
            _gqa_ctx_kernel,
            grid=(b,),
            in_specs=[sm, pl.BlockSpec((c, qw), lambda bi: (ctx_blk + bi, qc0)),
                      pl.BlockSpec((c, LANES), lambda bi: (ctx_blk + bi, kc0)),
                      pl.BlockSpec((c, LANES), lambda bi: (ctx_blk + bi, vc0))],
            out_specs=pl.BlockSpec((c, qw), lambda bi: (bi, 0)),
            out_shape=jax.ShapeDtypeStruct((b * c, qw), BF16),
            compiler_params=_cparams(("arbitrary",)),
            name="gqa_ctx",
        )(sink, p, p, p)
    tq = GQA_TQ
    nq = s // tq
    wpq = tq // WINDOW
    nwb = s // WINDOW

    def loc_specs(col):
        return [pl.BlockSpec((WINDOW, LANES), lambda bi, qi: (bi * nwb + jnp.maximum(qi * wpq - 1, 0), col)),
                pl.BlockSpec((tq, LANES), lambda bi, qi: (bi * nq + qi, col)),
                pl.BlockSpec((WINDOW, LANES), lambda bi, qi: (bi * nwb + jnp.minimum(qi * wpq + wpq, nwb - 1), col)),
                pl.BlockSpec((c, LANES), lambda bi, qi: (ctx_blk + bi, col))]

    return pl.pallas_call(
        functools.partial(_gqa_lat_kernel, seq=s),
        grid=(b, nq),
        in_specs=[sm, pl.BlockSpec((tq, qw), lambda bi, qi: (bi * nq + qi, qc0))] + loc_specs(kc0) + loc_specs(vc0),
        out_specs=pl.BlockSpec((tq, qw), lambda bi, qi: (bi * nq + qi, 0)),
        out_shape=jax.ShapeDtypeStruct((b * s, qw), BF16),
        compiler_params=_cparams(("arbitrary", "arbitrary")),
        name="gqa_lat",
    )(sink, p, *([p] * 8))


def _merge_kernel(x_ref, yr_ref, ydl_ref, ydc_ref, ygl_ref, ygc_ref, gmix_ref, gffn_ref, sh1_ref, sc1_ref, ga1_ref,
                  sh2_ref, sc2_ref, wg_ref, wb_ref, wo_ref, wrh_ref, wrl_ref, br_ref,
                  x1_ref, h2_ref, ti_ref, tw_ref, *, n_lat_tiles):
    d = x_ref.shape[1]
    x = x_ref[...]
    h = (_rms(x, gmix_ref[...]) * (1.0 + sc1_ref[0]) + sh1_ref[0]).astype(BF16)
    rw = RET_HEADS * RET_DV
    dw = DIFF_HEADS * LANES
    is_lat = pl.program_id(0) < n_lat_tiles
    yd = jnp.where(is_lat, ydl_ref[...], ydc_ref[...])
    yg = jnp.where(is_lat, ygl_ref[...], ygc_ref[...])
    merged = None
    for bi, (y, r0, r1) in enumerate(((yr_ref[...], 0, rw), (yd, rw, rw + dw), (yg, rw + dw, wb_ref.shape[0]))):
        gate = _sigmoid(_dot(h, wg_ref[:, bi * d:(bi + 1) * d]))
        term = gate * _dot(y, wb_ref[r0:r1, :])
        merged = term if merged is None else merged + term
    x1 = x + ga1_ref[0] * _dot(merged.astype(BF16), wo_ref[...])
    x1_ref[...] = x1
    h2 = _rms(x1, gffn_ref[...]) * (1.0 + sc2_ref[0]) + sh2_ref[0]
    tm, ns = x_ref.shape[0], d // LANES
    for s in range(ns):
        h2_ref[pl.ds(s, tm, stride=ns), :] = h2[:, s * LANES:(s + 1) * LANES]
    hh, hl = _split_bf16(h2)
    wrh, wrl = wrh_ref[...], wrl_ref[...]
    logits = _dot(hh, wrh) + (_dot(hl, wrh) + _dot(hh, wrl)) + br_ref[...]
    eidx = lax.broadcasted_iota(jnp.int32, logits.shape, 1).astype(F32)
    vals = []
    for k in range(TOP_K):
        mx = jnp.max(logits, axis=1, keepdims=True)
        ix = jnp.min(jnp.where(logits == mx, eidx, float(N_EXPERTS)), axis=1, keepdims=True)
        vals.append(mx)
        ti_ref[:, k:k + 1] = ix.astype(jnp.int32)
        logits = jnp.where(eidx == ix, -jnp.inf, logits)
    es = [jnp.exp(v - vals[0]) for v in vals]
    tot = es[0] + es[1] + es[2] + es[3]
    for k in range(TOP_K):
        tw_ref[:, k:k + 1] = es[k] / tot


def _merge(xall, yr, yd, yg, gmix, gffn, mod3, wg, wb, wo, wrh, wrl, br, dims, ntok):
    d = xall.shape[1]
    tm = dims["tm_merge"]
    tps = dims["seq"] // tm
    n_lat_tiles = dims["batch"] * tps

    def mrow(i):
        return jnp.minimum(i // tps, dims["batch"])

    def modspec(k):
        return pl.BlockSpec((1, 1, d), lambda i: (mrow(i), 0, k))

    def full(a):
        return pl.BlockSpec(a.shape, lambda i: (0,) * a.ndim)

    def rows(w):
        return pl.BlockSpec((tm, w), lambda i: (i, 0))

    def lat_ctx(pair):
        lat, ctx = pair
        w = lat.shape[1]
        lat_spec = pl.BlockSpec((tm, w), lambda i: (jnp.minimum(i, n_lat_tiles - 1), 0))
        if ctx is None:
            return [lat_spec, lat_spec], [lat, lat]
        n_ctx_tiles = ctx.shape[0] // tm
        ctx_spec = pl.BlockSpec((tm, w), lambda i: (jnp.clip(i - n_lat_tiles, 0, n_ctx_tiles - 1), 0))
        return [lat_spec, ctx_spec], [lat, ctx]

    yd_specs, yd_args = lat_ctx(yd)
    yg_specs, yg_args = lat_ctx(yg)
    return pl.pallas_call(
        functools.partial(_merge_kernel, n_lat_tiles=n_lat_tiles),
        grid=(ntok // tm,),
        in_specs=[rows(d), rows(yr.shape[1])] + yd_specs + yg_specs + [full(gmix), full(gffn),
                  modspec(0), modspec(1), modspec(2), modspec(3), modspec(4),
                  full(wg), full(wb), full(wo), full(wrh), full(wrl), full(br)],
        out_specs=[rows(d), pl.BlockSpec((tm * (d // LANES), LANES), lambda i: (i, 0)),
                   pl.BlockSpec((tm, TOP_K), lambda i: (i, 0)), pl.BlockSpec((tm, TOP_K), lambda i: (i, 0))],
        out_shape=[jax.ShapeDtypeStruct((ntok, d), F32), jax.ShapeDtypeStruct((ntok * (d // LANES), LANES), F32),
                   jax.ShapeDtypeStruct((ntok, TOP_K), jnp.int32), jax.ShapeDtypeStruct((ntok, TOP_K), F32)],
        compiler_params=_cparams(("arbitrary",)),
        name="merge",
    )(xall, yr, *yd_args, *yg_args, gmix, gffn, mod3, mod3, mod3, mod3, mod3, wg, wb, wo, wrh, wrl, br)


TOK_BITS = 15
FFN_CHUNK = 256
GATHER_DEPTH = 3


def _moe_kernel(be_ref, nv_ref, nu_ref, base_ref, code_ref, h2_hbm, wgu_ref, bgu_ref, wdn_ref, bdn_ref, o_hbm,
                xbuf, ybuf, wgu_bf, wdn_bf, gsem, ssem):
    i = pl.program_id(0)
    nu = nu_ref[0]
    slot = i % 2
    gslot = lax.rem(i, GATHER_DEPTH)
    last_blk = pl.num_programs(0) - 1
    nb = MOE_BLOCK
    d = wdn_bf.shape[0]
    ns = d // LANES

    def gather_rows(blk, sl, r0, r1):
        base = base_ref[blk]
        for r in range(r0, r1):
            tok = code_ref[base + r] & ((1 << TOK_BITS) - 1)
            pltpu.make_async_copy(h2_hbm.at[pl.ds(pl.multiple_of(tok * ns, ns), ns)], xbuf.at[sl, pl.ds(r * ns, ns)],
                                  gsem.at[sl]).start(priority=r % 2)

    def wait_rows(n, copy_of_rows):
        @pl.when(n == nb)
        def _():
            copy_of_rows(nb).wait()

        @pl.when(n != nb)
        def _():
            def body(r, carry):
                copy_of_rows(1).wait()
                return carry
            lax.fori_loop(0, n, body, 0)

    def gather_wait(sl):
        pltpu.make_async_copy(h2_hbm.at[pl.ds(0, nb * ns)], xbuf.at[sl], gsem.at[sl]).wait()

    def scatter_rows(blk, sl, r0, r1):
        base = base_ref[blk]
        n = nv_ref[blk]

        def body(r):
            dst = lax.shift_right_logical(code_ref[base + r], TOK_BITS)
            pltpu.make_async_copy(ybuf.at[sl, pl.ds(pl.multiple_of(r * ns, ns), ns)],
                                  o_hbm.at[pl.ds(pl.multiple_of(dst * ns, ns), ns)], ssem.at[sl]
                                  ).start(priority=r % 2 if isinstance(r, int) else 0)

        @pl.when(n == nb)
        def _():
            for r in range(r0, r1):
                body(r)

        @pl.when(n != nb)
        def _():
            def step(r, carry):
                body(r)
                return carry
            lax.fori_loop(r0, jnp.clip(n, r0, r1), step, 0)

    def scatter_wait(blk, sl):
        wait_rows(nv_ref[blk], lambda n: pltpu.make_async_copy(
            ybuf.at[sl, pl.ds(0, n * ns)], o_hbm.at[pl.ds(0, n * ns)], ssem.at[sl]))

    @pl.when(i < nu)
    def _():
        @pl.when(i == 0)
        def _():
            gather_rows(0, 0, 0, nb)
            gather_rows(jnp.minimum(1, last_blk), 1, 0, nb)

        @pl.when((i == 0) | (be_ref[i] != be_ref[jnp.maximum(i - 1, 0)]))
        def _():
            rc = 128

            def cast(r, carry):
                r0 = pl.multiple_of(r * rc, rc)
                wgu_bf[pl.ds(r0, rc), :] = wgu_ref[0, 0, pl.ds(r0, rc), :].astype(BF16)
                wdn_bf[pl.ds(r0, rc), :] = wdn_ref[0, 0, pl.ds(r0, rc), :].astype(BF16)
                return carry
            lax.fori_loop(0, d // rc, cast, 0)

        gather_wait(gslot)
        nxt = jnp.minimum(i + GATHER_DEPTH - 1, last_blk)
        nslot = lax.rem(i + GATHER_DEPTH - 1, GATHER_DEPTH)
        x = jnp.concatenate([xbuf[gslot, pl.ds(s, nb, stride=ns), :] for s in range(ns)], axis=1).astype(BF16)
        cw = FFN_CHUNK
        nch = d // cw
        y = None
        for ci in range(nch):
            c0, c1 = ci * cw, (ci + 1) * cw
            gather_rows(nxt, nslot, ci * (nb // nch), (ci + 1) * (nb // nch))
            glu = jnp.minimum(_dot(x, wgu_bf[:, c0:c1]) + bgu_ref[0, 0, :, c0:c1], SWIGLU_LIMIT)
            lin = jnp.clip(_dot(x, wgu_bf[:, d + c0:d + c1]) + bgu_ref[0, 0, :, d + c0:d + c1],
                           -SWIGLU_LIMIT, SWIGLU_LIMIT)
            act = glu * _sigmoid(SWIGLU_ALPHA * glu) * (lin + 1.0)
            part = _dot(act.astype(BF16), wdn_bf[c0:c1, :])
            y = part if y is None else y + part
        y = y + bdn_ref[0, 0]

        @pl.when(i >= 2)
        def _():
            scatter_wait(i - 2, slot)

        for s in range(ns):
            ybuf[slot, pl.ds(s, nb, stride=ns), :] = y[:, s * LANES:(s + 1) * LANES]
        scatter_rows(i, slot, 0, nb)

        @pl.when(i == nu - 1)
        def _():
            gather_wait(lax.rem(i + 1, GATHER_DEPTH))
            gather_wait(nslot)

            @pl.when(i >= 1)
            def _():
                scatter_wait(i - 1, 1 - slot)
            scatter_wait(i, slot)


def _moe(h2, route, w_gu, b_gu, w_dn, b_dn, layer):
    codes, block_expert, n_valid, n_used, base = route
    d = w_dn.shape[-1]
    ns = d // LANES
    ntok = h2.shape[0] // ns
    n_blocks = block_expert.shape[0]
    depth, ne = w_gu.shape[:2]

    def wspec(rows, cols):
        return pl.BlockSpec((1, 1, rows, cols), lambda i, be, nv, nu, bs, cd: (layer, be[i], 0, 0))

    grid_spec = pltpu.PrefetchScalarGridSpec(
        num_scalar_prefetch=5,
        grid=(n_blocks,),
        in_specs=[pl.BlockSpec(memory_space=pl.ANY),
                  wspec(d, 2 * d), wspec(1, 2 * d), wspec(d, d), wspec(1, d)],
        out_specs=pl.BlockSpec(memory_space=pl.ANY),
        scratch_shapes=[pltpu.VMEM((GATHER_DEPTH, MOE_BLOCK * ns, LANES), F32),
                        pltpu.VMEM((2, MOE_BLOCK * ns, LANES), F32),
                        pltpu.VMEM((d, 2 * d), BF16), pltpu.VMEM((d, d), BF16),
                        pltpu.SemaphoreType.DMA((GATHER_DEPTH,)), pltpu.SemaphoreType.DMA((2,))])
    return pl.pallas_call(
        _moe_kernel,
        grid_spec=grid_spec,
        out_shape=jax.ShapeDtypeStruct((TOP_K * ntok * ns, LANES), F32),
        compiler_params=_cparams(("arbitrary",)),
        name="moe",
    )(block_expert, n_valid, n_used, base, codes, h2, w_gu, b_gu.reshape(depth, ne, 1, 2 * d), w_dn,
      b_dn.reshape(depth, ne, 1, d))


def _route(ti, ntok):
    a = TOP_K * ntok
    n_blocks = (a + N_EXPERTS * (MOE_BLOCK - 1) + MOE_BLOCK - 1) // MOE_BLOCK
    expert = ti.T.reshape(a)
    aid_bits = (a - 1).bit_length()
    assert aid_bits + 5 < 32 and N_EXPERTS <= 32
    aid = jnp.arange(a, dtype=jnp.int32)
    aid_sorted = jnp.sort(lax.shift_left(expert, aid_bits) | aid, stable=False) & ((1 << aid_bits) - 1)
    codes = lax.shift_left(aid_sorted, TOK_BITS) | (aid_sorted % ntok)
    codes = jnp.concatenate([codes, jnp.zeros((MOE_BLOCK,), jnp.int32)])
    counts = jnp.sum((expert[:, None] == jnp.arange(N_EXPERTS, dtype=jnp.int32)[None, :]).astype(jnp.int32), axis=0)
    padded = (counts + MOE_BLOCK - 1) // MOE_BLOCK * MOE_BLOCK
    pad_end = jnp.cumsum(padded)
    pad_start = pad_end - padded
    cstart = jnp.cumsum(counts) - counts
    n_used = (pad_end[-1:] // MOE_BLOCK).astype(jnp.int32)
    block_start = jnp.arange(n_blocks, dtype=jnp.int32) * MOE_BLOCK
    block_expert = jnp.minimum(jnp.sum((pad_end[None, :] <= block_start[:, None]).astype(jnp.int32), axis=1),
                               N_EXPERTS - 1)
    n_valid = jnp.clip(counts[block_expert] - (block_start - pad_start[block_expert]), 0, MOE_BLOCK).astype(jnp.int32)
    base = jnp.clip(block_start - (pad_start - cstart)[block_expert], 0, a).astype(jnp.int32)
    return codes, block_expert, n_valid, n_used, base


def _combine_kernel(x1_ref, o0_ref, o1_ref, o2_ref, o3_ref, tw_ref, ga2_ref, gf_ref, x2_ref, *, final):
    o_refs = (o0_ref, o1_ref, o2_ref, o3_ref)
    tm, ns = x1_ref.shape[0], x1_ref.shape[1] // LANES
    w = [jnp.broadcast_to(tw_ref[:, k:k + 1], (tm, LANES)) for k in range(TOP_K)]
    for s in range(ns):
        cols = slice(s * LANES, (s + 1) * LANES)
        y = None
        for k in range(TOP_K):
            term = w[k] * o_refs[k][pl.ds(s, tm, stride=ns), :]
            y = term if y is None else y + term
        x2_ref[:, cols] = x1_ref[:, cols] + ga2_ref[0, :, cols] * y
    if final:
        x2_ref[...] = _rms(x2_ref[...], gf_ref[...])


def _combine(x1, o4, tw, mod3, g_final, dims, final):
    ntok, d = x1.shape
    tm = dims["tm_merge"]
    tps = dims["seq"] // tm
    nt = ntok // tm

    def ospec(k):
        return pl.BlockSpec((tm * (d // LANES), LANES), lambda i: (k * nt + i, 0))

    return pl.pallas_call(
        functools.partial(_combine_kernel, final=final),
        grid=(nt,),
        in_specs=[pl.BlockSpec((tm, d), lambda i: (i, 0)), ospec(0), ospec(1), ospec(2), ospec(3),
                  pl.BlockSpec((tm, TOP_K), lambda i: (i, 0)),
                  pl.BlockSpec((1, 1, d), lambda i: (jnp.minimum(i // tps, dims["batch"]), 0, 5)),
                  pl.BlockSpec((1, d), lambda i: (0, 0))],
        out_specs=pl.BlockSpec((tm, d), lambda i: (i, 0)),
        out_shape=jax.ShapeDtypeStruct((ntok, d), F32),
        compiler_params=_cparams(("arbitrary",)),
        name="combine",
    )(x1, o4, o4, o4, o4, tw, mod3, g_final)


def _rope_tables(seq, pad_rows):
    pos = jnp.arange(seq, dtype=jnp.int32)
    row = (pos // GRID_W).astype(F32)
    col = (pos % GRID_W).astype(F32)

    def cs(head_dim):
        quarter = head_dim // 4
        inv_freq = ROPE_BASE ** (-jnp.arange(quarter, dtype=F32) / quarter)
        ang = jnp.concatenate([row[:, None] * inv_freq, col[:, None] * inv_freq], axis=-1)
        return jnp.cos(ang), jnp.sin(ang)

    c128, s128 = cs(RET_DK)
    c64, s64 = cs(DIFF_DK)
    ones = jnp.ones((pad_rows, LANES), F32)
    zeros = jnp.zeros((pad_rows, LANES), F32)
    a128 = jnp.concatenate([jnp.concatenate([c128, c128], -1), ones], 0)
    b128 = jnp.concatenate([jnp.concatenate([-s128, s128], -1), zeros], 0)
    a64 = jnp.concatenate([jnp.concatenate([c64, c64, c64, c64], -1), ones], 0)
    b64 = jnp.concatenate([jnp.concatenate([-s64, s64, -s64, s64], -1), zeros], 0)
    return a128, b128, a64, b64


def _pick_tile(cands, *sizes):
    for t in cands:
        if all(s % t == 0 for s in sizes):
            return t
    raise ValueError("no tile size fits")


def kernel(x, c, ctx, c_ctx, w_mod, b_mod, g_norm_mix, g_norm_ffn, w_in, ret_decay_fwd, ret_decay_bwd, diff_lambda_q1, diff_lambda_k1, diff_lambda_q2, diff_lambda_k2, diff_subln_g, gqa_sink, w_branch, w_out, w_router, b_router, w_gate_up, b_gate_up, w_down, b_down, g_final):
    b, s, d = x.shape
    n_ctx = ctx.shape[1]
    depth = w_mod.shape[0]
    n_lat = b * s
    tt = n_lat + b * n_ctx
    assert s % RET_CHUNK == 0 and n_ctx % RET_CHUNK == 0 and s % GQA_TQ == 0 and n_lat % n_ctx == 0
    assert tt < (1 << TOK_BITS) and b < 8
    tm = _pick_tile((1024, 512, 256), s, b * n_ctx)
    tm_merge = _pick_tile((512, 256), s, b * n_ctx)
    dims = dict(batch=b, seq=s, ctx=n_ctx, tm=tm, tm_merge=tm_merge,
                n_lat_tiles=n_lat // tm, tiles_per_seq=s // tm)

    xall = jnp.concatenate([x.reshape(n_lat, d), ctx.reshape(b * n_ctx, d)], axis=0)
    cvec = jnp.concatenate([c, c_ctx[None, :], jnp.zeros((8 - b - 1, d), F32)], axis=0)
    mod = _mod_all(cvec, w_mod, b_mod)
    tabs = _rope_tables(s, tm)

    for l in range(depth):
        last = l == depth - 1
        mod3 = mod[l].reshape(8, 1, 6 * d)
        w_l = w_in[l]
        p = _inproj(xall, g_norm_mix[l][None, :], mod3, w_in, l, tabs, dims)

        lg = jnp.stack([jax.nn.log_sigmoid(ret_decay_fwd[l].astype(F32)),
                        jax.nn.log_sigmoid(ret_decay_bwd[l].astype(F32))], axis=0)
        y_r = _retention(p, lg, dims)

        lam_init = 0.8 - 0.6 * math.exp(-0.3 * l)
        lam = (jnp.exp(jnp.sum(diff_lambda_q1[l].astype(F32) * diff_lambda_k1[l].astype(F32)))
               - jnp.exp(jnp.sum(diff_lambda_q2[l].astype(F32) * diff_lambda_k2[l].astype(F32))) + lam_init)
        scal = jnp.stack([lam, jnp.asarray(1.0 - lam_init, F32)])
        subln = diff_subln_g[l][None, :].astype(F32)
        sink = gqa_sink[l].astype(F32)
        y_d = (_diff_attention(p, scal, subln, dims, False),
               None if last else _diff_attention(p, scal, subln, dims, True))
        y_g = (_gqa(p, sink, dims, False), None if last else _gqa(p, sink, dims, True))
        ntok = n_lat if last else tt

        wrh, wrl = _split_bf16(w_router[l])
        x1, h2, ti, tw = _merge(xall, y_r, y_d, y_g, g_norm_mix[l][None, :], g_norm_ffn[l][None, :], mod3,
                                w_l[:, P_COLS:].astype(BF16), w_branch[l].astype(BF16), w_out[l].astype(BF16),
                                wrh, wrl, b_router[l][None, :].astype(F32), dims, ntok)
        o4 = _moe(h2, _route(ti, ntok), w_gate_up, b_gate_up, w_down, b_down, l)
        xall = _combine(x1, o4, tw, mod3, g_final[None, :], dims, last)
    return xall.reshape(b, s, d)
```
